```python
import math
import jax, jax.numpy as jnp
from jax import lax
import numpy as np

D_MODEL = 2048
BATCH = 1
SEQ = 8192
DEPTH = 2

HEAD_DIM = 64
D_PLE = 256
BLOCK = 128
WINDOW = 128
RMS_EPS = 1e-6
A_HEADS = 16
A_KV_HEADS = 2
A_GROUP = A_HEADS // A_KV_HEADS
A_WIDTH = A_HEADS * HEAD_DIM
A_KV_WIDTH = A_KV_HEADS * HEAD_DIM
B_HEADS = 8
B_VDIM = 2 * HEAD_DIM
B_WIDTH = B_HEADS * B_VDIM
C_HEADS = 32
C_WIDTH = C_HEADS * HEAD_DIM
N_EVEN = (DEPTH + 1) // 2
N_ODD = DEPTH // 2
AB_SPLITS = (A_WIDTH, A_KV_WIDTH, A_KV_WIDTH, A_WIDTH,
             B_HEADS * 2 * HEAD_DIM, B_HEADS * 2 * HEAD_DIM, B_WIDTH, B_WIDTH)
AB_IN = sum(AB_SPLITS)
AB_OUT = A_WIDTH + B_WIDTH
C_SPLITS = (C_WIDTH, C_WIDTH, C_WIDTH, C_HEADS, C_WIDTH)
C_IN = sum(C_SPLITS)

kernel_name = "hybrid_swa_sink_diff_fox_gated_ple"


def rmsnorm(x, g):
    xf = x.astype(jnp.float32)
    y = xf * lax.rsqrt(jnp.mean(xf * xf, axis=-1, keepdims=True) + RMS_EPS)
    return (y * g.astype(jnp.float32)).astype(x.dtype)


def split_cols(z, sizes):
    idx = np.cumsum(sizes)[:-1].tolist()
    return jnp.split(z, idx, axis=-1)


def alibi_slopes(n):
    return jnp.asarray([2.0 ** (-8.0 * (h + 1) / n) for h in range(n)], dtype=jnp.float32)


def sliding_window_sink_attn(q, k, v, sinks):
    b, s = q.shape[0], q.shape[1]
    nb = s // BLOCK
    qb = q.reshape(b, nb, BLOCK, A_KV_HEADS, A_GROUP, HEAD_DIM)

    def band(t):
        prev = jnp.pad(t, ((0, 0), (BLOCK, 0), (0, 0), (0, 0)))[:, :s]
        return jnp.concatenate([prev.reshape(b, nb, BLOCK, A_KV_HEADS, HEAD_DIM),
                                t.reshape(b, nb, BLOCK, A_KV_HEADS, HEAD_DIM)], axis=2)

    kb, vb = band(k), band(v)
    scores = jnp.einsum('bnikgd,bnjkd->bnkgij', qb, kb).astype(jnp.float32) * (HEAD_DIM ** -0.5)
    i = jnp.arange(BLOCK)[:, None]
    j = jnp.arange(2 * BLOCK)[None, :]
    dist = i - j + BLOCK
    key_pos = jnp.arange(nb)[:, None, None] * BLOCK - BLOCK + j[None]
    valid = (dist >= 0) & (dist < WINDOW) & (key_pos >= 0)
    slopes = alibi_slopes(A_HEADS).reshape(A_KV_HEADS, A_GROUP)[:, :, None, None]
    scores = scores - slopes * dist.astype(jnp.float32)
    scores = jnp.where(valid[None, :, None, None], scores, -jnp.inf)
    sink = sinks.astype(jnp.float32).reshape(A_KV_HEADS, A_GROUP)[:, :, None, None]
    m = jnp.maximum(scores.max(axis=-1, keepdims=True), sink)
    e = jnp.exp(scores - m)
    probs = e / (e.sum(axis=-1, keepdims=True) + jnp.exp(sink - m))
    out = jnp.einsum('bnkgij,bnjkd->bnikgd', probs.astype(v.dtype), vb)
    return out.reshape(b, s, A_WIDTH)


def diff_attn(q, k, v, lam, subln_g, lam_init):
    b, s = q.shape[0], q.shape[1]
    nb = s // BLOCK
    qb = jnp.moveaxis(q.reshape(b, nb, BLOCK, B_HEADS, 2, HEAD_DIM), 1, 0)
    slopes = alibi_slopes(B_HEADS)[:, None, None]
    kpos = jnp.arange(s)
    scale = HEAD_DIM ** -0.5

    def one_block(args):
        qblk, n = args
        qpos = n * BLOCK + jnp.arange(BLOCK)
        dist = (qpos[:, None] - kpos[None, :]).astype(jnp.float32)
        scores = jnp.einsum('bihcd,bjhcd->bchij', qblk, k).astype(jnp.float32) * scale
        scores = jnp.where(dist >= 0, scores - slopes * dist, -jnp.inf)
        pr = jax.nn.softmax(scores, axis=-1)
        w = pr[:, 0] - lam * pr[:, 1]
        return jnp.einsum('bhij,bjhe->bihe', w.astype(v.dtype), v)

    out = lax.map(one_block, (qb, jnp.arange(nb)))
    out = jnp.moveaxis(out, 0, 1).reshape(b, s, B_HEADS, B_VDIM)
    out = rmsnorm(out, subln_g) * (1.0 - lam_init)
    return out.reshape(b, s, B_WIDTH)


def forgetting_attn(q, k, v, f_logit):
    b, s = q.shape[0], q.shape[1]
    nb = s // BLOCK
    c = jnp.cumsum(jax.nn.log_sigmoid(f_logit.astype(jnp.float32)), axis=1)
    cT = jnp.moveaxis(c, 1, 2)
    qb = jnp.moveaxis(q.reshape(b, nb, BLOCK, C_HEADS, HEAD_DIM), 1, 0)
    cb = jnp.moveaxis(cT.reshape(b, C_HEADS, nb, BLOCK), 2, 0)
    kpos = jnp.arange(s)
    scale = HEAD_DIM ** -0.5

    def one_block(args):
        qblk, cq, n = args
        qpos = n * BLOCK + jnp.arange(BLOCK)
        causal = qpos[:, None] >= kpos[None, :]
        scores = (jnp.einsum('bihd,bjhd->bhij', qblk, k).astype(jnp.float32) * scale
                  + cq[..., None] - cT[:, :, None, :])
        scores = jnp.where(causal, scores, -jnp.inf)
        pr = jax.nn.softmax(scores, axis=-1)
        return jnp.einsum('bhij,bjhd->bihd', pr.astype(v.dtype), v)

    out = lax.map(one_block, (qb, cb, jnp.arange(nb)))
    return jnp.moveaxis(out, 0, 1).reshape(b, s, C_WIDTH)


def mixer_ab(h, w_in, w_out, sinks, lam_params, subln_g, lam_init):
    b, s, _ = h.shape
    z = h @ w_in
    qa, ka, va, ga, qd, kd, vd, gd = split_cols(z, AB_SPLITS)
    ya = sliding_window_sink_attn(qa.reshape(b, s, A_HEADS, HEAD_DIM),
                                  ka.reshape(b, s, A_KV_HEADS, HEAD_DIM),
                                  va.reshape(b, s, A_KV_HEADS, HEAD_DIM), sinks)
    ya = ya * jax.nn.silu(ga)
    lp = lam_params.astype(jnp.float32)
    lam = jnp.exp(jnp.sum(lp[0] * lp[1])) - jnp.exp(jnp.sum(lp[2] * lp[3])) + lam_init
    yb = diff_attn(qd.reshape(b, s, B_HEADS, 2, HEAD_DIM),
                   kd.reshape(b, s, B_HEADS, 2, HEAD_DIM),
                   vd.reshape(b, s, B_HEADS, B_VDIM), lam, subln_g, lam_init)
    yb = yb * jax.nn.silu(gd)
    return jnp.concatenate([ya, yb], axis=-1) @ w_out


def mixer_c(h, w_in, w_out, f_bias):
    b, s, _ = h.shape
    z = h @ w_in
    q, k, v, fz, g = split_cols(z, C_SPLITS)
    y = forgetting_attn(q.reshape(b, s, C_HEADS, HEAD_DIM),
                        k.reshape(b, s, C_HEADS, HEAD_DIM),
                        v.reshape(b, s, C_HEADS, HEAD_DIM), fz + f_bias)
    return (y * jax.nn.silu(g)) @ w_out


def setup_inputs(seed: int = 0) -> dict:
    key = jax.random.key(seed)
    ks = jax.random.split(key, 16)
    f32 = jnp.float32
    nrm = lambda k, shape, scale: jax.random.normal(k, shape, f32) * scale
    return {
        "x": nrm(ks[0], (BATCH, SEQ, D_MODEL), 1.0),
        "p": nrm(ks[1], (DEPTH, BATCH, SEQ, D_PLE), 1.0),
        "norm_g": 1.0 + nrm(ks[2], (DEPTH, D_MODEL), 0.02),
        "w_in_ab": nrm(ks[3], (N_EVEN, D_MODEL, AB_IN), D_MODEL ** -0.5),
        "w_out_ab": nrm(ks[4], (N_EVEN, AB_OUT, D_MODEL), AB_OUT ** -0.5),
        "attn_sinks": nrm(ks[5], (N_EVEN, A_HEADS), 0.5),
        "diff_lambda": nrm(ks[6], (N_EVEN, 4, HEAD_DIM), 0.1),
        "diff_subln_g": 1.0 + nrm(ks[7], (N_EVEN, B_VDIM), 0.02),
        "w_in_c": nrm(ks[8], (N_ODD, D_MODEL, C_IN), D_MODEL ** -0.5),
        "w_out_c": nrm(ks[9], (N_ODD, C_WIDTH, D_MODEL), C_WIDTH ** -0.5),
        "forget_bias": 3.0 + nrm(ks[10], (N_ODD, C_HEADS), 1.0),
        "ple_proj": nrm(ks[11], (DEPTH, D_PLE, D_MODEL), D_PLE ** -0.5),
        "ple_gate": nrm(ks[12], (DEPTH, D_MODEL, D_MODEL), D_MODEL ** -0.5),
        "ple_norm_g": 1.0 + nrm(ks[13], (DEPTH, D_MODEL), 0.02),
        "final_norm_g": 1.0 + nrm(ks[14], (D_MODEL,), 0.02),
    }


def reference(x, p, norm_g, w_in_ab, w_out_ab, attn_sinks, diff_lambda, diff_subln_g,
              w_in_c, w_out_c, forget_bias, ple_proj, ple_gate, ple_norm_g, final_norm_g):
    for i in range(DEPTH):
        h = rmsnorm(x, norm_g[i])
        j = i // 2
        if i % 2 == 0:
            lam_init = 0.8 - 0.6 * math.exp(-0.3 * i)
            y = mixer_ab(h, w_in_ab[j], w_out_ab[j], attn_sinks[j], diff_lambda[j],
                         diff_subln_g[j], lam_init)
        else:
            y = mixer_c(h, w_in_c[j], w_out_c[j], forget_bias[j])
        x = x + y
        gate = jax.nn.sigmoid(rmsnorm(x, ple_norm_g[i]) @ ple_gate[i])
        x = x + gate * (p[i] @ ple_proj[i])
    return rmsnorm(x, final_norm_g)
```

```python
import functools
import math

import jax
import jax.numpy as jnp
from jax import lax
from jax.experimental import pallas as pl
from jax.experimental.pallas import tpu as pltpu

F32 = jnp.float32
BF16 = jnp.bfloat16

D_MODEL = 2048
SEQ = 8192
DEPTH = 2
HEAD_DIM = 64
D_PLE = 256
WINDOW = 128
RMS_EPS = 1e-6
A_HEADS = 16
A_KV_HEADS = 2
A_GROUP = A_HEADS // A_KV_HEADS
A_WIDTH = A_HEADS * HEAD_DIM
B_HEADS = 8
B_WIDTH = B_HEADS * 2 * HEAD_DIM
C_HEADS = 32
C_WIDTH = C_HEADS * HEAD_DIM
AB_IN = 6400
SCALE = HEAD_DIM ** -0.5

LANES = 128
NEG_INF = float("-inf")
VMEM_LIMIT = 48 * 1024 * 1024


def _cparams(sem):
    return pltpu.CompilerParams(dimension_semantics=sem, vmem_limit_bytes=VMEM_LIMIT)


def _silu(g):
    return g * (1.0 / (1.0 + jnp.exp(-g)))


def _norm_matmul_kernel(x_ref, g_ref, w_ref, *rest, with_extra):
    if with_extra:
        we_ref, o_ref, oe_ref, h_ref = rest
    else:
        o_ref, h_ref = rest

    @pl.when(pl.program_id(1) == 0)
    def _():
        x = x_ref[...]
        ms = jnp.mean(x * x, axis=-1, keepdims=True)
        h = (x * lax.rsqrt(ms + RMS_EPS) * g_ref[...]).astype(BF16)
        h_ref[...] = h
        if with_extra:
            oe_ref[...] = jnp.dot(h, we_ref[...], preferred_element_type=F32)

    o_ref[...] = jnp.dot(h_ref[...], w_ref[...], preferred_element_type=F32).astype(o_ref.dtype)


def _norm_matmul(x, g, w, w_extra=None, *, tm, tn):
    s, d = x.shape
    n = w.shape[1]
    with_extra = w_extra is not None
    in_specs = [
        pl.BlockSpec((tm, d), lambda i, j: (i, 0)),
        pl.BlockSpec((1, d), lambda i, j: (0, 0)),
        pl.BlockSpec((d, tn), lambda i, j: (0, j)),
    ]
    out_shape = [jax.ShapeDtypeStruct((s, n), BF16)]
    out_specs = [pl.BlockSpec((tm, tn), lambda i, j: (i, j))]
    args = [x, g.reshape(1, d), w]
    if with_extra:
        ne = w_extra.shape[1]
        in_specs.append(pl.BlockSpec((d, ne), lambda i, j: (0, 0)))
        out_shape.append(jax.ShapeDtypeStruct((s, ne), F32))
        out_specs.append(pl.BlockSpec((tm, ne), lambda i, j: (i, 0)))
        args.append(w_extra)
    res = pl.pallas_call(
        functools.partial(_norm_matmul_kernel, with_extra=with_extra),
        grid=(s // tm, n // tn),
        in_specs=in_specs,
        out_specs=out_specs,
        out_shape=out_shape,
        scratch_shapes=[pltpu.VMEM((tm, d), BF16)],
        compiler_params=_cparams(("parallel", "arbitrary")),
    )(*args)
    return res if with_extra else res[0]


def _logsig_cumsum_kernel(f_ref, b_ref, o_ref, carry_ref, *, blk):
    @pl.when(pl.program_id(0) == 0)
    def _():
        carry_ref[...] = jnp.zeros_like(carry_ref)

    z = f_ref[...] + b_ref[...]
    ls = jnp.minimum(z, 0.0) - jnp.log1p(jnp.exp(-jnp.abs(z)))
    row = lax.broadcasted_iota(jnp.int32, (blk, blk), 0)
    col = lax.broadcasted_iota(jnp.int32, (blk, blk), 1)
    upper = (row <= col).astype(F32)
    c = jnp.dot(ls.T, upper, preferred_element_type=F32,
                precision=lax.Precision.HIGHEST) + carry_ref[...]
    o_ref[...] = c
    carry_ref[...] = c[:, blk - 1:blk]


def _logsig_cumsum(f, bias_row, *, blk=256):
    s, n = f.shape
    return pl.pallas_call(
        functools.partial(_logsig_cumsum_kernel, blk=blk),
        grid=(s // blk,),
        in_specs=[pl.BlockSpec((blk, n), lambda i: (i, 0)),
                  pl.BlockSpec((1, n), lambda i: (0, 0))],
        out_specs=pl.BlockSpec((n, blk), lambda i: (0, i)),
        out_shape=jax.ShapeDtypeStruct((n, s), F32),
        scratch_shapes=[pltpu.VMEM((n, 1), F32)],
        compiler_params=_cparams(("arbitrary",)),
    )(f, bias_row)


def _pair_flash(q_ref, k_ref, v_ref, bias_fn, i, *, tq, tk):
    lane = lax.broadcasted_iota(jnp.int32, (1, LANES), 1)
    lo = lane < HEAD_DIM
    q = q_ref[...] * jnp.asarray(SCALE, BF16)
    zero = jnp.zeros_like(q)
    qs = (jnp.where(lo, q, zero), jnp.where(lo, zero, q))

    def tile(j0, carry, masked):
        k = k_ref[pl.ds(j0, tk), :]
        v = v_ref[pl.ds(j0, tk), :]
        out = []
        for h in range(2):
            m, l, acc = carry[h]
            s = lax.dot_general(qs[h], k, (((1,), (1,)), ((), ())),
                                preferred_element_type=F32)
            t = s + bias_fn(h, j0)
            if masked:
                r = lax.broadcasted_iota(jnp.int32, (tq, tk), 0)
                c = lax.broadcasted_iota(jnp.int32, (tq, tk), 1)
                t = jnp.where(r >= c, t, NEG_INF)
            m_new = jnp.maximum(m, jnp.max(t, axis=-1, keepdims=True))
            alpha = jnp.exp(m - m_new)
            p = jnp.exp(t - m_new)
            l = alpha * l + jnp.sum(p, axis=-1, keepdims=True)
            acc = alpha * acc + jnp.dot(p.astype(BF16), v, preferred_element_type=F32)
            out.append((m_new, l, acc))
        return tuple(out)

    init = tuple((jnp.full((tq, 1), NEG_INF, F32), jnp.zeros((tq, 1), F32),
                  jnp.zeros((tq, LANES), F32)) for _ in range(2))

    carry = tile(pl.multiple_of(i * tq, tq), init, True)
    carry = lax.fori_loop(0, i, lambda j, c: tile(pl.multiple_of(j * tk, tk), c, False), carry)
    (_, l_lo, acc_lo), (_, l_hi, acc_hi) = carry
    return acc_lo, l_lo, acc_hi, l_hi


def _fox_kernel(q_ref, k_ref, v_ref, g_ref, c_ref, o_ref, *, tq, tk):
    i = pl.program_id(1)

    def bias_fn(h, j0):
        return -c_ref[h:h + 1, pl.ds(j0, tk)]

    acc_lo, l_lo, acc_hi, l_hi = _pair_flash(q_ref, k_ref, v_ref, bias_fn, i, tq=tq, tk=tk)
    lane = lax.broadcasted_iota(jnp.int32, (1, LANES), 1)
    lo = lane < HEAD_DIM
    y = jnp.where(lo, acc_lo / l_lo, acc_hi / l_hi)
    o_ref[...] = (y * _silu(g_ref[...].astype(F32))).astype(o_ref.dtype)


def _fox_attention(z, c_pairs, *, tq):
    s = z.shape[0]
    nb = C_WIDTH // LANES
    return pl.pallas_call(
        functools.partial(_fox_kernel, tq=tq, tk=tq),
        grid=(nb, s // tq),
        in_specs=[
            pl.BlockSpec((tq, LANES), lambda p, i: (i, p)),
            pl.BlockSpec((s, LANES), lambda p, i: (0, nb + p)),
            pl.BlockSpec((s, LANES), lambda p, i: (0, 2 * nb + p)),
            pl.BlockSpec((tq, LANES), lambda p, i: (i, 3 * nb + p)),
            pl.BlockSpec((None, 2, s), lambda p, i: (p, 0, 0)),
        ],
        out_specs=pl.BlockSpec((tq, LANES), lambda p, i: (i, p)),
        out_shape=jax.ShapeDtypeStruct((s, C_WIDTH), BF16),
        compiler_params=_cparams(("parallel", "arbitrary")),
    )(z, z, z, z, c_pairs)


def _diff_kernel(q_ref, k_ref, v_ref, g_ref, slope_ref, lam_ref, sg_ref, o_ref, *, tq, tk, lam_init):
    i = pl.program_id(1)
    slope = slope_ref[...]
    col = lax.broadcasted_iota(jnp.int32, (1, tk), 1)

    def bias_fn(h, j0):
        return slope * (col + (j0 - i * tq)).astype(F32)

    acc1, l1, acc2, l2 = _pair_flash(q_ref, k_ref, v_ref, bias_fn, i, tq=tq, tk=tk)
    lp = lam_ref[...]
    lam = (jnp.exp(jnp.sum(lp[0:1] * lp[1:2], axis=-1, keepdims=True))
           - jnp.exp(jnp.sum(lp[2:3] * lp[3:4], axis=-1, keepdims=True)) + lam_init)
    o = acc1 / l1 - lam * (acc2 / l2)
    o = o * lax.rsqrt(jnp.mean(o * o, axis=-1, keepdims=True) + RMS_EPS) * sg_ref[...]
    o = o * (1.0 - lam_init)
    o_ref[...] = (o * _silu(g_ref[...].astype(F32))).astype(o_ref.dtype)


def _diff_attention(z, slopes, lam_params, subln_g, lam_init, *, tq):
    s = z.shape[0]
    qb, kb, vb, gb = 18, 26, 34, 42
    return pl.pallas_call(
        functools.partial(_diff_kernel, tq=tq, tk=tq, lam_init=lam_init),
        grid=(B_HEADS, s // tq),
        in_specs=[
            pl.BlockSpec((tq, LANES), lambda h, i: (i, qb + h)),
            pl.BlockSpec((s, LANES), lambda h, i: (0, kb + h)),
            pl.BlockSpec((s, LANES), lambda h, i: (0, vb + h)),
            pl.BlockSpec((tq, LANES), lambda h, i: (i, gb + h)),
            pl.BlockSpec((None, 1, 1), lambda h, i: (h, 0, 0)),
            pl.BlockSpec((4, HEAD_DIM), lambda h, i: (0, 0)),
            pl.BlockSpec((1, LANES), lambda h, i: (0, 0)),
        ],
        out_specs=pl.BlockSpec((tq, LANES), lambda h, i: (i, h)),
        out_shape=jax.ShapeDtypeStruct((s, B_WIDTH), BF16),
        compiler_params=_cparams(("parallel", "arbitrary")),
    )(z, z, z, z, slopes.reshape(B_HEADS, 1, 1), lam_params, subln_g.reshape(1, LANES))


def _swa_kernel(slope_ref, sink_ref, q_ref, kp_ref, kc_ref, vp_ref, vc_ref, g_ref, o_ref):
    i = pl.program_id(0)
    blk = WINDOW
    lane = lax.broadcasted_iota(jnp.int32, (1, LANES), 1)
    lo = lane < HEAD_DIM

    def swap_halves(a):
        return pltpu.roll(a.astype(F32), HEAD_DIM, 1).astype(BF16)

    k = jnp.concatenate([kp_ref[...], kc_ref[...]], axis=0)
    v = jnp.concatenate([vp_ref[...], vc_ref[...]], axis=0)
    k_sw, v_sw = swap_halves(k), swap_halves(v)

    r = lax.broadcasted_iota(jnp.int32, (blk, 2 * blk), 0)
    c = lax.broadcasted_iota(jnp.int32, (blk, 2 * blk), 1)
    dist = r - c + blk
    valid = (dist >= 0) & (dist < WINDOW) & ((c >= blk) | (i > 0))
    dist_f = dist.astype(F32)

    for pair in range(A_HEADS // 2):
        qp = q_ref[:, pair * LANES:(pair + 1) * LANES] * jnp.asarray(SCALE, BF16)
        zero = jnp.zeros_like(qp)
        outs = []
        for a in range(2):
            h = 2 * pair + a
            kv = h // A_GROUP
            qh = jnp.where(lo, qp, zero) if a == 0 else jnp.where(lo, zero, qp)
            kk, vv = (k, v) if a == kv else (k_sw, v_sw)
            s = lax.dot_general(qh, kk, (((1,), (1,)), ((), ())), preferred_element_type=F32)
            s = jnp.where(valid, s - slope_ref[h] * dist_f, NEG_INF)
            sink = sink_ref[h]
            m = jnp.maximum(jnp.max(s, axis=-1, keepdims=True), sink)
            e = jnp.exp(s - m)
            denom = jnp.sum(e, axis=-1, keepdims=True) + jnp.exp(sink - m)
            p = (e / denom).astype(BF16)
            outs.append(jnp.dot(p, vv, preferred_element_type=F32))
        y = jnp.where(lo, outs[0], outs[1])
        g = g_ref[:, pair * LANES:(pair + 1) * LANES].astype(F32)
        o_ref[:, pair * LANES:(pair + 1) * LANES] = (y * _silu(g)).astype(o_ref.dtype)


def _swa_attention(z, slopes, sinks):
    s = z.shape[0]
    blk = WINDOW
    nq = A_WIDTH // LANES
    kb, vb = nq, nq + 1
    g_col = (nq + 2) * LANES
    smem = pl.BlockSpec(memory_space=pltpu.SMEM)
    prev = lambda i: (jnp.maximum(i - 1, 0), kb)
    prev_v = lambda i: (jnp.maximum(i - 1, 0), vb)
    return pl.pallas_call(
        _swa_kernel,
        grid=(s // blk,),
        in_specs=[
            smem, smem,
            pl.BlockSpec((blk, A_WIDTH), lambda i: (i, 0)),
            pl.BlockSpec((blk, LANES), prev),
            pl.BlockSpec((blk, LANES), lambda i: (i, kb)),
            pl.BlockSpec((blk, LANES), prev_v),
            pl.BlockSpec((blk, LANES), lambda i: (i, vb)),
            pl.BlockSpec((blk, A_WIDTH), lambda i: (i, 0)),
        ],
        out_specs=pl.BlockSpec((blk, A_WIDTH), lambda i: (i, 0)),
        out_shape=jax.ShapeDtypeStruct((s, A_WIDTH), BF16),
        compiler_params=_cparams(("parallel",)),
    )(slopes, sinks, z, z, z, z, z, z[:, g_col:g_col + A_WIDTH])


def _out_kernel(*refs, n_y, final):
    x_ref = refs[0]
    y_refs = refs[1:1 + n_y]
    w_refs = refs[1 + n_y:1 + 2 * n_y]
    p_ref, wp_ref, gn_ref, wg_ref = refs[1 + 2 * n_y:5 + 2 * n_y]
    if final:
        fg_ref, o_ref = refs[5 + 2 * n_y:]
    else:
        (o_ref,) = refs[5 + 2 * n_y:]

    x1 = x_ref[...]
    for y_ref, w_ref in zip(y_refs, w_refs):
        x1 = x1 + jnp.dot(y_ref[...], w_ref[...], preferred_element_type=F32)
    hn = x1 * lax.rsqrt(jnp.mean(x1 * x1, axis=-1, keepdims=True) + RMS_EPS) * gn_ref[...]
    gate = jax.nn.sigmoid(jnp.dot(hn.astype(BF16), wg_ref[...], preferred_element_type=F32))
    pp = jnp.dot(p_ref[...].astype(BF16), wp_ref[...], preferred_element_type=F32)
    x2 = x1 + gate * pp
    if final:
        x2 = x2 * lax.rsqrt(jnp.mean(x2 * x2, axis=-1, keepdims=True) + RMS_EPS) * fg_ref[...]
    o_ref[...] = x2


def _out_block(x, ys, ws, p, wp, gn, wg, final_g=None, *, tm):
    s, d = x.shape
    n_y = len(ys)
    final = final_g is not None
    const = lambda i: (0, 0)
    single = pl.Buffered(1)
    in_specs = [pl.BlockSpec((tm, d), lambda i: (i, 0))]
    in_specs += [pl.BlockSpec((tm, y.shape[1]), lambda i: (i, 0)) for y in ys]
    in_specs += [pl.BlockSpec(w.shape, const, pipeline_mode=single) for w in ws]
    in_specs += [
        pl.BlockSpec((tm, p.shape[1]), lambda i: (i, 0)),
        pl.BlockSpec(wp.shape, const, pipeline_mode=single),
        pl.BlockSpec((1, d), const),
        pl.BlockSpec(wg.shape, const, pipeline_mode=single),
    ]
    args = [x, *ys, *ws, p, wp, gn.reshape(1, d), wg]
    if final:
        in_specs.append(pl.BlockSpec((1, d), const))
        args.append(final_g.reshape(1, d))
    return pl.pallas_call(
        functools.partial(_out_kernel, n_y=n_y, final=final),
        grid=(s // tm,),
        in_specs=in_specs,
        out_specs=pl.BlockSpec((tm, d), lambda i: (i, 0)),
        out_shape=jax.ShapeDtypeStruct((s, d), F32),
        compiler_params=_cparams(("parallel",)),
    )(*args)


def _alibi_slopes(n):
    return jnp.asarray([2.0 ** (-8.0 * (h + 1) / n) for h in range(n)], dtype=F32)


def kernel(x, p, norm_g, w_in_ab, w_out_ab, attn_sinks, diff_lambda, diff_subln_g,
           w_in_c, w_out_c, forget_bias, ple_proj, ple_gate, ple_norm_g, final_norm_g):
    b, s, d = x.shape
    assert (b, s, d) == (1, SEQ, D_MODEL)
    xs = x.reshape(s, d)
    for i in range(DEPTH):
        j = i // 2
        last = i == DEPTH - 1
        if i % 2 == 0:
            lam_init = 0.8 - 0.6 * math.exp(-0.3 * i)
            z = _norm_matmul(xs, norm_g[i], w_in_ab[j].astype(BF16), tm=512, tn=1280)
            ya = _swa_attention(z, _alibi_slopes(A_HEADS), attn_sinks[j].astype(F32))
            yb = _diff_attention(z, _alibi_slopes(B_HEADS), diff_lambda[j].astype(F32),
                                 diff_subln_g[j].astype(F32), lam_init, tq=512)
            w_out = w_out_ab[j].astype(BF16)
            ys, ws = [ya, yb], [w_out[:A_WIDTH], w_out[A_WIDTH:]]
        else:
            w = w_in_c[j]
            w_main = jnp.concatenate([w[:, :3 * C_WIDTH], w[:, 3 * C_WIDTH + C_HEADS:]], axis=1).astype(BF16)
            w_f = jnp.pad(w[:, 3 * C_WIDTH:3 * C_WIDTH + C_HEADS], ((0, 0), (0, LANES - C_HEADS))).astype(BF16)
            z, f = _norm_matmul(xs, norm_g[i], w_main, w_f, tm=512, tn=1024)
            f_bias = jnp.pad(forget_bias[j].astype(F32), (0, LANES - C_HEADS)).reshape(1, LANES)
            c_t = _logsig_cumsum(f, f_bias)
            c_pairs = c_t[:C_HEADS].reshape(C_HEADS // 2, 2, s)
            ys, ws = [_fox_attention(z, c_pairs, tq=512)], [w_out_c[j].astype(BF16)]
        xs = _out_block(xs, ys, ws, p[i].reshape(s, D_PLE), ple_proj[i].astype(BF16),
                        ple_norm_g[i], ple_gate[i].astype(BF16),
                        final_norm_g if last else None, tm=256)
    return xs.reshape(b, s, d)
```

```python
import functools
import math

import jax
import jax.numpy as jnp
from jax import lax
from jax.experimental import pallas as pl
from jax.experimental.pallas import tpu as pltpu

F32 = jnp.float32
BF16 = jnp.bfloat16

D_MODEL = 2048
SEQ = 8192
DEPTH = 2
HEAD_DIM = 64
D_PLE = 256
WINDOW = 128
RMS_EPS = 1e-6
A_HEADS = 16
A_KV_HEADS = 2
A_GROUP = A_HEADS // A_KV_HEADS
A_WIDTH = A_HEADS * HEAD_DIM
B_HEADS = 8
B_WIDTH = B_HEADS * 2 * HEAD_DIM
C_HEADS = 32
C_WIDTH = C_HEADS * HEAD_DIM
AB_IN = 6400
SCALE = HEAD_DIM ** -0.5

LANES = 128
NEG_INF = float("-inf")
VMEM_LIMIT = 48 * 1024 * 1024


def _cparams(sem):
    return pltpu.CompilerParams(dimension_semantics=sem, vmem_limit_bytes=VMEM_LIMIT)


def _silu(g):
    return g * (1.0 / (1.0 + jnp.exp(-g)))


def _norm_matmul_kernel(x_ref, g_ref, w_ref, *rest, with_extra):
    if with_extra:
        we_ref, o_ref, oe_ref, h_ref = rest
    else:
        o_ref, h_ref = rest

    @pl.when(pl.program_id(1) == 0)
    def _():
        x = x_ref[...]
        ms = jnp.mean(x * x, axis=-1, keepdims=True)
        h = (x * lax.rsqrt(ms + RMS_EPS) * g_ref[...]).astype(BF16)
        h_ref[...] = h
        if with_extra:
            oe_ref[...] = jnp.dot(h, we_ref[...], preferred_element_type=F32)

    o_ref[...] = jnp.dot(h_ref[...], w_ref[...], preferred_element_type=F32).astype(o_ref.dtype)


def _norm_matmul(x, g, w, w_extra=None, *, tm, tn):
    s, d = x.shape
    n = w.shape[1]
    with_extra = w_extra is not None
    in_specs = [
        pl.BlockSpec((tm, d), lambda i, j: (i, 0)),
        pl.BlockSpec((1, d), lambda i, j: (0, 0)),
        pl.BlockSpec((d, tn), lambda i, j: (0, j)),
    ]
    out_shape = [jax.ShapeDtypeStruct((s, n), BF16)]
    out_specs = [pl.BlockSpec((tm, tn), lambda i, j: (i, j))]
    args = [x, g.reshape(1, d), w]
    if with_extra:
        ne = w_extra.shape[1]
        in_specs.append(pl.BlockSpec((d, ne), lambda i, j: (0, 0)))
        out_shape.append(jax.ShapeDtypeStruct((s, ne), F32))
        out_specs.append(pl.BlockSpec((tm, ne), lambda i, j: (i, 0)))
        args.append(w_extra)
    res = pl.pallas_call(
        functools.partial(_norm_matmul_kernel, with_extra=with_extra),
        grid=(s // tm, n // tn),
        in_specs=in_specs,
        out_specs=out_specs,
        out_shape=out_shape,
        scratch_shapes=[pltpu.VMEM((tm, d), BF16)],
        compiler_params=_cparams(("parallel", "arbitrary")),
    )(*args)
    return res if with_extra else res[0]


def _logsig_cumsum_kernel(f_ref, b_ref, o_ref, carry_ref, *, blk):
    @pl.when(pl.program_id(0) == 0)
    def _():
        carry_ref[...] = jnp.zeros_like(carry_ref)

    z = f_ref[...] + b_ref[...]
    ls = jnp.minimum(z, 0.0) - jnp.log1p(jnp.exp(-jnp.abs(z)))
    row = lax.broadcasted_iota(jnp.int32, (blk, blk), 0)
    col = lax.broadcasted_iota(jnp.int32, (blk, blk), 1)
    lower = (col <= row).astype(F32)
    c = jnp.dot(lower, ls, preferred_element_type=F32,
                precision=lax.Precision.HIGHEST) + carry_ref[...]
    o_ref[...] = c
    carry_ref[...] = c[blk - 1:blk, :]


def _logsig_cumsum(f, bias_row, *, blk=256):
    s, n = f.shape
    return pl.pallas_call(
        functools.partial(_logsig_cumsum_kernel, blk=blk),
        grid=(s // blk,),
        in_specs=[pl.BlockSpec((blk, n), lambda i: (i, 0)),
                  pl.BlockSpec((1, n), lambda i: (0, 0))],
        out_specs=pl.BlockSpec((blk, n), lambda i: (i, 0)),
        out_shape=jax.ShapeDtypeStruct((s, n), F32),
        scratch_shapes=[pltpu.VMEM((1, n), F32)],
        compiler_params=_cparams(("arbitrary",)),
    )(f, bias_row)


def _fill_vt(v_ref, vt_ref, tk):
    def body(c, carry):
        vt_ref[c] = v_ref[pl.ds(pl.multiple_of(c * tk, tk), tk), :].T
        return carry
    lax.fori_loop(0, vt_ref.shape[0], body, 0)


def _flash_scratch(s, tq, tk, rows):
    return [
        pltpu.VMEM((s // tk, LANES, tk), BF16),
        pltpu.VMEM((2, LANES, tq), BF16),
        pltpu.VMEM((2, 2, tk, tq), F32),
        pltpu.VMEM((2, 2, tk, tq), BF16),
        pltpu.VMEM((2, 1, tq), F32),
        pltpu.VMEM((2, 1, tq), F32),
        pltpu.VMEM((2, 1, tq), F32),
        pltpu.VMEM((2, rows, tq), F32),
    ]


def _pair_flash(q_ref, k_ref, col_bias, i, scratch, *, tq, tk, v_rows):
    assert tq == tk
    vt_ref, qt_ref, s_ref, p_ref, m_ref, l_ref, a_ref, acc_ref = scratch
    row = lax.broadcasted_iota(jnp.int32, (LANES, 1), 0)
    lo = row < HEAD_DIM
    qt = (q_ref[...] * jnp.asarray(SCALE, BF16)).T
    zero = jnp.zeros_like(qt)
    qt_ref[0] = jnp.where(lo, qt, zero)
    qt_ref[1] = jnp.where(lo, zero, qt)
    for h in range(2):
        m_ref[h] = jnp.full((1, tq), NEG_INF, F32)
        l_ref[h] = jnp.zeros((1, tq), F32)
        a_ref[h] = jnp.ones((1, tq), F32)
        acc_ref[h] = jnp.zeros(acc_ref.shape[1:], F32)
        p_ref[0, h] = jnp.zeros((tk, tq), BF16)

    def scores(j, slot, masked):
        j0 = pl.multiple_of(j * tk, tk)
        k = k_ref[pl.ds(j0, tk), :]
        for h in range(2):
            t = jnp.dot(k, qt_ref[h], preferred_element_type=F32) + col_bias(h, j0)
            if masked:
                kpos = lax.broadcasted_iota(jnp.int32, (tk, tq), 0)
                qpos = lax.broadcasted_iota(jnp.int32, (tk, tq), 1)
                t = jnp.where(kpos <= qpos, t, NEG_INF)
            s_ref[slot, h] = t

    def softmax(slot_s, slot_p):
        for h in range(2):
            m_old = m_ref[h]
            m_new = jnp.maximum(m_old, jnp.max(s_ref[slot_s, h], axis=0, keepdims=True))
            alpha = jnp.exp(m_old - m_new)
            p = jnp.exp(s_ref[slot_s, h] - m_new)
            l_ref[h] = alpha * l_ref[h] + jnp.sum(p, axis=0, keepdims=True)
            m_ref[h] = m_new
            a_ref[h] = alpha
            p_ref[slot_p, h] = p.astype(BF16)

    def values(j, slot_p):
        for h in range(2):
            r0, r1 = v_rows[h]
            acc_ref[h] = acc_ref[h] * a_ref[h] + jnp.dot(vt_ref[j, r0:r1, :], p_ref[slot_p, h],
                                                        preferred_element_type=F32)

    def order(n):
        return jnp.where(n == 0, i, jnp.maximum(n - 1, 0))

    scores(i, 0, True)

    def step(t, slot):
        values(order(t - 1), slot)
        softmax(slot, 1 - slot)
        scores(t, 1 - slot, False)

    def drain(slot):
        values(order(i - 1), slot)
        softmax(slot, 1 - slot)
        values(order(i), 1 - slot)

    def two_steps(u, carry):
        step(2 * u, 0)
        step(2 * u + 1, 1)
        return carry

    lax.fori_loop(0, i // 2, two_steps, 0)

    @pl.when(i % 2 == 1)
    def _():
        step(i - 1, 0)
        drain(1)

    @pl.when(i % 2 == 0)
    def _():
        drain(0)

    return acc_ref, l_ref


def _fox_kernel(q_ref, k_ref, v_ref, g_ref, c_ref, o_ref, *scratch, tq, tk):
    i = pl.program_id(1)

    @pl.when(i == 0)
    def _():
        _fill_vt(v_ref, scratch[0], tk)

    def col_bias(h, j0):
        return -c_ref[pl.ds(j0, tk), h:h + 1]

    acc_ref, l_ref = _pair_flash(q_ref, k_ref, col_bias, i, scratch, tq=tq, tk=tk,
                                 v_rows=((0, HEAD_DIM), (HEAD_DIM, LANES)))
    y = jnp.concatenate([acc_ref[0] / l_ref[0], acc_ref[1] / l_ref[1]], axis=0).T
    o_ref[...] = (y * _silu(g_ref[...].astype(F32))).astype(o_ref.dtype)


def _fox_attention(z, c_pairs, *, tq, tk):
    s = z.shape[0]
    nb = C_WIDTH // LANES
    return pl.pallas_call(
        functools.partial(_fox_kernel, tq=tq, tk=tk),
        grid=(nb, s // tq),
        in_specs=[
            pl.BlockSpec((tq, LANES), lambda p, i: (i, p)),
            pl.BlockSpec((s, LANES), lambda p, i: (0, nb + p)),
            pl.BlockSpec((s, LANES), lambda p, i: (0, 2 * nb + p)),
            pl.BlockSpec((tq, LANES), lambda p, i: (i, 3 * nb + p)),
            pl.BlockSpec((None, s, 2), lambda p, i: (p, 0, 0)),
        ],
        out_specs=pl.BlockSpec((tq, LANES), lambda p, i: (i, p)),
        out_shape=jax.ShapeDtypeStruct((s, C_WIDTH), BF16),
        scratch_shapes=_flash_scratch(s, tq, tk, HEAD_DIM),
        compiler_params=_cparams(("parallel", "arbitrary")),
    )(z, z, z, z, c_pairs)


def _diff_kernel(q_ref, k_ref, v_ref, g_ref, slope_ref, lam_ref, sg_ref, o_ref, *scratch, tq, tk, lam_init):
    i = pl.program_id(1)

    @pl.when(i == 0)
    def _():
        _fill_vt(v_ref, scratch[0], tk)

    slope = slope_ref[...]
    rel = lax.broadcasted_iota(jnp.int32, (tk, 1), 0) - i * tq

    def col_bias(h, j0):
        return slope * (rel + j0).astype(F32)

    acc_ref, l_ref = _pair_flash(q_ref, k_ref, col_bias, i, scratch, tq=tq, tk=tk,
                                 v_rows=((0, LANES), (0, LANES)))
    lp = lam_ref[...]
    lam = (jnp.exp(jnp.sum(lp[0:1] * lp[1:2], axis=-1, keepdims=True))
           - jnp.exp(jnp.sum(lp[2:3] * lp[3:4], axis=-1, keepdims=True)) + lam_init)
    o = (acc_ref[0] / l_ref[0] - lam * (acc_ref[1] / l_ref[1])).T
    o = o * lax.rsqrt(jnp.mean(o * o, axis=-1, keepdims=True) + RMS_EPS) * sg_ref[...]
    o = o * (1.0 - lam_init)
    o_ref[...] = (o * _silu(g_ref[...].astype(F32))).astype(o_ref.dtype)


def _diff_attention(z, slopes, lam_params, subln_g, lam_init, *, tq, tk):
    s = z.shape[0]
    qb, kb, vb, gb = 18, 26, 34, 42
    return pl.pallas_call(
        functools.partial(_diff_kernel, tq=tq, tk=tk, lam_init=lam_init),
        grid=(B_HEADS, s // tq),
        in_specs=[
            pl.BlockSpec((tq, LANES), lambda h, i: (i, qb + h)),
            pl.BlockSpec((s, LANES), lambda h, i: (0, kb + h)),
            pl.BlockSpec((s, LANES), lambda h, i: (0, vb + h)),
            pl.BlockSpec((tq, LANES), lambda h, i: (i, gb + h)),
            pl.BlockSpec((None, 1, 1), lambda h, i: (h, 0, 0)),
            pl.BlockSpec((4, HEAD_DIM), lambda h, i: (0, 0)),
            pl.BlockSpec((1, LANES), lambda h, i: (0, 0)),
        ],
        out_specs=pl.BlockSpec((tq, LANES), lambda h, i: (i, h)),
        out_shape=jax.ShapeDtypeStruct((s, B_WIDTH), BF16),
        scratch_shapes=_flash_scratch(s, tq, tk, LANES),
        compiler_params=_cparams(("parallel", "arbitrary")),
    )(z, z, z, z, slopes.reshape(B_HEADS, 1, 1), lam_params, subln_g.reshape(1, LANES))


def _swa_kernel(slope_ref, sink_ref, q_ref, kp_ref, kc_ref, vp_ref, vc_ref, g_ref, o_ref):
    i = pl.program_id(0)
    blk = WINDOW
    lane = lax.broadcasted_iota(jnp.int32, (1, LANES), 1)
    lo = lane < HEAD_DIM

    def swap_halves(a):
        return pltpu.roll(a.astype(F32), HEAD_DIM, 1).astype(BF16)

    k = jnp.concatenate([kp_ref[...], kc_ref[...]], axis=0)
    v = jnp.concatenate([vp_ref[...], vc_ref[...]], axis=0)
    k_sw, v_sw = swap_halves(k), swap_halves(v)

    r = lax.broadcasted_iota(jnp.int32, (blk, 2 * blk), 0)
    c = lax.broadcasted_iota(jnp.int32, (blk, 2 * blk), 1)
    dist = r - c + blk
    valid = (dist >= 0) & (dist < WINDOW) & ((c >= blk) | (i > 0))
    dist_f = dist.astype(F32)

    for pair in range(A_HEADS // 2):
        qp = q_ref[:, pair * LANES:(pair + 1) * LANES] * jnp.asarray(SCALE, BF16)
        zero = jnp.zeros_like(qp)
        outs = []
        for a in range(2):
            h = 2 * pair + a
            kv = h // A_GROUP
            qh = jnp.where(lo, qp, zero) if a == 0 else jnp.where(lo, zero, qp)
            kk, vv = (k, v) if a == kv else (k_sw, v_sw)
            s = lax.dot_general(qh, kk, (((1,), (1,)), ((), ())), preferred_element_type=F32)
            s = jnp.where(valid, s - slope_ref[h] * dist_f, NEG_INF)
            sink = sink_ref[h]
            m = jnp.maximum(jnp.max(s, axis=-1, keepdims=True), sink)
            e = jnp.exp(s - m)
            denom = jnp.sum(e, axis=-1, keepdims=True) + jnp.exp(sink - m)
            p = (e / denom).astype(BF16)
            outs.append(jnp.dot(p, vv, preferred_element_type=F32))
        y = jnp.where(lo, outs[0], outs[1])
        g = g_ref[:, pair * LANES:(pair + 1) * LANES].astype(F32)
        o_ref[:, pair * LANES:(pair + 1) * LANES] = (y * _silu(g)).astype(o_ref.dtype)


def _swa_attention(z, slopes, sinks):
    s = z.shape[0]
    blk = WINDOW
    nq = A_WIDTH // LANES
    kb, vb = nq, nq + 1
    g_col = (nq + 2) * LANES
    smem = pl.BlockSpec(memory_space=pltpu.SMEM)
    prev = lambda i: (jnp.maximum(i - 1, 0), kb)
    prev_v = lambda i: (jnp.maximum(i - 1, 0), vb)
    return pl.pallas_call(
        _swa_kernel,
        grid=(s // blk,),
        in_specs=[
            smem, smem,
            pl.BlockSpec((blk, A_WIDTH), lambda i: (i, 0)),
            pl.BlockSpec((blk, LANES), prev),
            pl.BlockSpec((blk, LANES), lambda i: (i, kb)),
            pl.BlockSpec((blk, LANES), prev_v),
            pl.BlockSpec((blk, LANES), lambda i: (i, vb)),
            pl.BlockSpec((blk, A_WIDTH), lambda i: (i, 0)),
        ],
        out_specs=pl.BlockSpec((blk, A_WIDTH), lambda i: (i, 0)),
        out_shape=jax.ShapeDtypeStruct((s, A_WIDTH), BF16),
        compiler_params=_cparams(("parallel",)),
    )(slopes, sinks, z, z, z, z, z, z[:, g_col:g_col + A_WIDTH])


def _out_kernel(*refs, n_y, final):
    x_ref = refs[0]
    y_refs = refs[1:1 + n_y]
    w_refs = refs[1 + n_y:1 + 2 * n_y]
    p_ref, wp_ref, gn_ref, wg_ref = refs[1 + 2 * n_y:5 + 2 * n_y]
    if final:
        fg_ref, o_ref = refs[5 + 2 * n_y:]
    else:
        (o_ref,) = refs[5 + 2 * n_y:]

    x1 = x_ref[...]
    for y_ref, w_ref in zip(y_refs, w_refs):
        x1 = x1 + jnp.dot(y_ref[...], w_ref[...], preferred_element_type=F32)
    hn = x1 * lax.rsqrt(jnp.mean(x1 * x1, axis=-1, keepdims=True) + RMS_EPS) * gn_ref[...]
    gate = jax.nn.sigmoid(jnp.dot(hn.astype(BF16), wg_ref[...], preferred_element_type=F32))
    pp = jnp.dot(p_ref[...].astype(BF16), wp_ref[...], preferred_element_type=F32)
    x2 = x1 + gate * pp
    if final:
        x2 = x2 * lax.rsqrt(jnp.mean(x2 * x2, axis=-1, keepdims=True) + RMS_EPS) * fg_ref[...]
    o_ref[...] = x2


def _out_block(x, ys, ws, p, wp, gn, wg, final_g=None, *, tm):
    s, d = x.shape
    n_y = len(ys)
    final = final_g is not None
    const = lambda i: (0, 0)
    single = pl.Buffered(1)
    in_specs = [pl.BlockSpec((tm, d), lambda i: (i, 0))]
    in_specs += [pl.BlockSpec((tm, y.shape[1]), lambda i: (i, 0)) for y in ys]
    in_specs += [pl.BlockSpec(w.shape, const, pipeline_mode=single) for w in ws]
    in_specs += [
        pl.BlockSpec((tm, p.shape[1]), lambda i: (i, 0)),
        pl.BlockSpec(wp.shape, const, pipeline_mode=single),
        pl.BlockSpec((1, d), const),
        pl.BlockSpec(wg.shape, const, pipeline_mode=single),
    ]
    args = [x, *ys, *ws, p, wp, gn.reshape(1, d), wg]
    if final:
        in_specs.append(pl.BlockSpec((1, d), const))
        args.append(final_g.reshape(1, d))
    return pl.pallas_call(
        functools.partial(_out_kernel, n_y=n_y, final=final),
        grid=(s // tm,),
        in_specs=in_specs,
        out_specs=pl.BlockSpec((tm, d), lambda i: (i, 0)),
        out_shape=jax.ShapeDtypeStruct((s, d), F32),
        compiler_params=_cparams(("parallel",)),
    )(*args)


def _alibi_slopes(n):
    return jnp.asarray([2.0 ** (-8.0 * (h + 1) / n) for h in range(n)], dtype=F32)


def kernel(x, p, norm_g, w_in_ab, w_out_ab, attn_sinks, diff_lambda, diff_subln_g,
           w_in_c, w_out_c, forget_bias, ple_proj, ple_gate, ple_norm_g, final_norm_g):
    b, s, d = x.shape
    assert (b, s, d) == (1, SEQ, D_MODEL)
    xs = x.reshape(s, d)
    for i in range(DEPTH):
        j = i // 2
        last = i == DEPTH - 1
        if i % 2 == 0:
            lam_init = 0.8 - 0.6 * math.exp(-0.3 * i)
            z = _norm_matmul(xs, norm_g[i], w_in_ab[j].astype(BF16), tm=512, tn=1280)
            ya = _swa_attention(z, _alibi_slopes(A_HEADS), attn_sinks[j].astype(F32))
            yb = _diff_attention(z, _alibi_slopes(B_HEADS), diff_lambda[j].astype(F32),
                                 diff_subln_g[j].astype(F32), lam_init, tq=512, tk=512)
            w_out = w_out_ab[j].astype(BF16)
            ys, ws = [ya, yb], [w_out[:A_WIDTH], w_out[A_WIDTH:]]
        else:
            w = w_in_c[j]
            w_main = jnp.concatenate([w[:, :3 * C_WIDTH], w[:, 3 * C_WIDTH + C_HEADS:]], axis=1).astype(BF16)
            w_f = jnp.pad(w[:, 3 * C_WIDTH:3 * C_WIDTH + C_HEADS], ((0, 0), (0, LANES - C_HEADS))).astype(BF16)
            z, f = _norm_matmul(xs, norm_g[i], w_main, w_f, tm=512, tn=1024)
            f_bias = jnp.pad(forget_bias[j].astype(F32), (0, LANES - C_HEADS)).reshape(1, LANES)
            c = _logsig_cumsum(f, f_bias)
            c_pairs = c[:, :C_HEADS].reshape(s, C_HEADS // 2, 2).transpose(1, 0, 2)
            ys, ws = [_fox_attention(z, c_pairs, tq=512, tk=512)], [w_out_c[j].astype(BF16)]
        xs = _out_block(xs, ys, ws, p[i].reshape(s, D_PLE), ple_proj[i].astype(BF16),
                        ple_norm_g[i], ple_gate[i].astype(BF16),
                        final_norm_g if last else None, tm=256)
    return xs.reshape(b, s, d)
```

```python
import functools
import math

import jax
import jax.numpy as jnp
from jax import lax
from jax.experimental import pallas as pl
from jax.experimental.pallas import tpu as pltpu

F32 = jnp.float32
BF16 = jnp.bfloat16

D_MODEL = 2048
SEQ = 8192
DEPTH = 2
HEAD_DIM = 64
D_PLE = 256
WINDOW = 128
RMS_EPS = 1e-6
A_HEADS = 16
A_KV_HEADS = 2
A_GROUP = A_HEADS // A_KV_HEADS
A_WIDTH = A_HEADS * HEAD_DIM
B_HEADS = 8
B_WIDTH = B_HEADS * 2 * HEAD_DIM
C_HEADS = 32
C_WIDTH = C_HEADS * HEAD_DIM
AB_IN = 6400
SCALE = HEAD_DIM ** -0.5

LANES = 128
NEG_INF = float("-inf")
VMEM_LIMIT = 48 * 1024 * 1024


def _cparams(sem):
    return pltpu.CompilerParams(dimension_semantics=sem, vmem_limit_bytes=VMEM_LIMIT)


def _silu(g):
    return g * (1.0 / (1.0 + jnp.exp(-g)))


def _norm_matmul_kernel(x_ref, g_ref, w_ref, *rest, with_extra):
    if with_extra:
        we_ref, o_ref, oe_ref, h_ref = rest
    else:
        o_ref, h_ref = rest

    @pl.when(pl.program_id(1) == 0)
    def _():
        x = x_ref[...]
        ms = jnp.mean(x * x, axis=-1, keepdims=True)
        h = (x * lax.rsqrt(ms + RMS_EPS) * g_ref[...]).astype(BF16)
        h_ref[...] = h
        if with_extra:
            oe_ref[...] = jnp.dot(h, we_ref[...], preferred_element_type=F32)

    o_ref[...] = jnp.dot(h_ref[...], w_ref[...], preferred_element_type=F32).astype(o_ref.dtype)


def _norm_matmul(x, g, w, w_extra=None, *, tm, tn):
    s, d = x.shape
    n = w.shape[1]
    with_extra = w_extra is not None
    in_specs = [
        pl.BlockSpec((tm, d), lambda i, j: (i, 0)),
        pl.BlockSpec((1, d), lambda i, j: (0, 0)),
        pl.BlockSpec((d, tn), lambda i, j: (0, j)),
    ]
    out_shape = [jax.ShapeDtypeStruct((s, n), BF16)]
    out_specs = [pl.BlockSpec((tm, tn), lambda i, j: (i, j))]
    args = [x, g.reshape(1, d), w]
    if with_extra:
        ne = w_extra.shape[1]
        in_specs.append(pl.BlockSpec((d, ne), lambda i, j: (0, 0)))
        out_shape.append(jax.ShapeDtypeStruct((s, ne), F32))
        out_specs.append(pl.BlockSpec((tm, ne), lambda i, j: (i, 0)))
        args.append(w_extra)
    res = pl.pallas_call(
        functools.partial(_norm_matmul_kernel, with_extra=with_extra),
        grid=(s // tm, n // tn),
        in_specs=in_specs,
        out_specs=out_specs,
        out_shape=out_shape,
        scratch_shapes=[pltpu.VMEM((tm, d), BF16)],
        compiler_params=_cparams(("parallel", "arbitrary")),
    )(*args)
    return res if with_extra else res[0]


N_SPLIT = 3


def _split_bf16(x):
    parts = []
    for _ in range(N_SPLIT):
        t = x.astype(BF16)
        parts.append(t)
        x = x - t.astype(F32)
    return parts


def _logsig_cumsum_kernel(f_ref, b_ref, o_ref, carry_ref, sel_ref, *, blk):
    n = f_ref.shape[1]

    @pl.when(pl.program_id(0) == 0)
    def _():
        carry_ref[...] = jnp.zeros_like(carry_ref)
        r = lax.broadcasted_iota(jnp.int32, sel_ref.shape, 0)
        l = lax.broadcasted_iota(jnp.int32, sel_ref.shape, 1)
        part, head = r // n, r % n
        lane = jnp.where(l // LANES == head // 2, l % LANES, -1)
        sel_ref[...] = jnp.where(lane == N_SPLIT * (head % 2) + part, 1.0, 0.0).astype(BF16)

    z = f_ref[...] + b_ref[...]
    ls = jnp.minimum(z, 0.0) - jnp.log1p(jnp.exp(-jnp.abs(z)))
    row = lax.broadcasted_iota(jnp.int32, (blk, blk), 0)
    col = lax.broadcasted_iota(jnp.int32, (blk, blk), 1)
    lower = (col <= row).astype(F32)
    c = jnp.dot(lower, ls, preferred_element_type=F32,
                precision=lax.Precision.HIGHEST) + carry_ref[...]
    carry_ref[...] = c[blk - 1:blk, :]
    terms = jnp.concatenate(_split_bf16(c), axis=1)
    o_ref[...] = jnp.dot(terms, sel_ref[...], preferred_element_type=F32).astype(BF16)


def _logsig_cumsum(f, bias_row, n_pairs, *, blk=256):
    s, n = f.shape
    return pl.pallas_call(
        functools.partial(_logsig_cumsum_kernel, blk=blk),
        grid=(s // blk,),
        in_specs=[pl.BlockSpec((blk, n), lambda i: (i, 0)),
                  pl.BlockSpec((1, n), lambda i: (0, 0))],
        out_specs=pl.BlockSpec((blk, n_pairs * LANES), lambda i: (i, 0)),
        out_shape=jax.ShapeDtypeStruct((s, n_pairs * LANES), BF16),
        scratch_shapes=[pltpu.VMEM((1, n), F32), pltpu.VMEM((N_SPLIT * n, n_pairs * LANES), BF16)],
        compiler_params=_cparams(("arbitrary",)),
    )(f, bias_row)


ONES_ROWS = 16
CHUNK = 256


def _flash_scratch(s, tq, tk, rows):
    return [
        pltpu.VMEM((s // tk, LANES + ONES_ROWS, tk), BF16),
        pltpu.VMEM((2, 2 * LANES, tq), BF16),
        pltpu.VMEM((2, 1, tq), F32),
        pltpu.VMEM((2, rows + ONES_ROWS, tq), F32),
        *[pltpu.VMEM((tk, tq), F32) for _ in range(4)],
        *[pltpu.VMEM((1, tq), F32) for _ in range(4)],
        *[pltpu.VMEM((tk, tq), BF16) for _ in range(4)],
        *[pltpu.VMEM((1, tq), F32) for _ in range(4)],
    ]


def _fill_vt(v_ref, vt_ref, tk):
    def body(c, carry):
        vt_ref[c, :LANES, :] = v_ref[pl.ds(pl.multiple_of(c * tk, tk), tk), :].T
        vt_ref[c, LANES:, :] = jnp.ones((ONES_ROWS, tk), BF16)
        return carry
    lax.fori_loop(0, vt_ref.shape[0], body, 0)


def _pair_flash(q_ref, k_ref, kb_fn, off_fn, i, scratch, *, tq, tk, v_rows, bias_rows, bias_sign):
    assert tq == 2 * tk
    vt_ref, qt_ref, m_ref, acc_ref = scratch[:4]
    s_ref, mt_ref, p_ref, a_ref = ({(slot, h): scratch[4 + 4 * n + 2 * slot + h]
                                    for slot in range(2) for h in range(2)} for n in range(4))
    row = lax.broadcasted_iota(jnp.int32, (LANES, 1), 0)
    lo = row < HEAD_DIM
    qt = (q_ref[...] * jnp.asarray(SCALE, BF16)).T
    zero = jnp.zeros_like(qt)
    for h in range(2):
        qt_ref[h, :LANES, :] = jnp.where(lo, qt, zero) if h == 0 else jnp.where(lo, zero, qt)
        b0, b1 = bias_rows[h]
        qt_ref[h, LANES:, :] = jnp.where((row >= b0) & (row < b1), bias_sign, 0.0).astype(BF16) + zero
        m_ref[h] = jnp.full((1, tq), NEG_INF, F32)
        acc_ref[h] = jnp.zeros(acc_ref.shape[1:], F32)

    def scores(j, slot, diag, h, cs):
        j0 = pl.multiple_of(j * tk, tk)
        lhs = jnp.concatenate([k_ref[pl.ds(j0, tk), :], kb_fn(j0)], axis=1)
        t = jnp.dot(lhs, qt_ref[h, :, cs], preferred_element_type=F32)
        if diag is not None:
            kpos = lax.broadcasted_iota(jnp.int32, t.shape, 0) + diag * tk
            qpos = lax.broadcasted_iota(jnp.int32, t.shape, 1) + cs.start
            t = jnp.where(kpos <= qpos, t, NEG_INF)
        s_ref[slot, h][:, cs] = t
        mt_ref[slot, h][:, cs] = jnp.max(t, axis=0, keepdims=True)

    def softmax(j, slot, h, cs):
        off = off_fn(j)
        m_old = m_ref[h, :, cs]
        mt = mt_ref[slot, h][:, cs]
        m_new = jnp.maximum(m_old, mt if off is None else mt + off)
        a_ref[1 - slot, h][:, cs] = jnp.exp(m_old - m_new)
        m_ref[h, :, cs] = m_new
        shift = m_new if off is None else m_new - off
        p_ref[1 - slot, h][:, cs] = jnp.exp(s_ref[slot, h][:, cs] - shift).astype(BF16)

    def values(j, slot, h, cs):
        r0, r1 = v_rows[h]
        vt = vt_ref[j, r0:r1, :]
        if r1 != LANES + ONES_ROWS:
            vt = jnp.concatenate([vt, vt_ref[j, LANES:, :]], axis=0)
        acc_ref[h, :, cs] = acc_ref[h, :, cs] * a_ref[slot, h][:, cs] + jnp.dot(
            vt, p_ref[slot, h][:, cs], preferred_element_type=F32)

    def tile_of(n):
        return jnp.where(n < 2, 2 * i + n, n - 2)

    lanes = [(h, slice(c, c + CHUNK)) for c in range(0, tq, CHUNK) for h in range(2)]

    def step(t, slot, diag=None, first=False, last=False):
        for h, cs in lanes:
            if not first:
                values(tile_of(t - 1), slot, h, cs)
            softmax(tile_of(t), slot, h, cs)
            if not last:
                scores(tile_of(t + 1), 1 - slot, diag, h, cs)

    for h, cs in lanes:
        scores(2 * i, 0, 0, h, cs)
    step(0, 0, 1, first=True)

    def two_steps(u, carry):
        step(2 * u + 1, 1)
        step(2 * u + 2, 0)
        return carry

    lax.fori_loop(0, i, two_steps, 0)
    n_last = 2 * i + 1
    step(n_last, 1, last=True)
    for h, cs in lanes:
        values(tile_of(n_last), 0, h, cs)
    return acc_ref


def _fox_kernel(q_ref, k_ref, v_ref, g_ref, cb_ref, o_ref, *scratch, tq, tk):
    i = pl.program_id(1)

    @pl.when(i == 0)
    def _():
        _fill_vt(v_ref, scratch[0], tk)

    acc_ref = _pair_flash(q_ref, k_ref, lambda j0: cb_ref[pl.ds(j0, tk), :], lambda j: None, i, scratch,
                          tq=tq, tk=tk, v_rows=((0, HEAD_DIM), (HEAD_DIM, LANES)),
                          bias_rows=((0, N_SPLIT), (N_SPLIT, 2 * N_SPLIT)), bias_sign=-1.0)
    d = HEAD_DIM
    y = jnp.concatenate([acc_ref[0, :d] / acc_ref[0, d:d + 1], acc_ref[1, :d] / acc_ref[1, d:d + 1]], axis=0).T
    o_ref[...] = (y * _silu(g_ref[...].astype(F32))).astype(o_ref.dtype)


def _fox_attention(z, cb, *, tq, tk):
    s = z.shape[0]
    nb = C_WIDTH // LANES
    return pl.pallas_call(
        functools.partial(_fox_kernel, tq=tq, tk=tk),
        grid=(nb, s // tq),
        in_specs=[
            pl.BlockSpec((tq, LANES), lambda p, i: (i, p)),
            pl.BlockSpec((s, LANES), lambda p, i: (0, nb + p)),
            pl.BlockSpec((s, LANES), lambda p, i: (0, 2 * nb + p)),
            pl.BlockSpec((tq, LANES), lambda p, i: (i, 3 * nb + p)),
            pl.BlockSpec((s, LANES), lambda p, i: (0, p)),
        ],
        out_specs=pl.BlockSpec((tq, LANES), lambda p, i: (i, p)),
        out_shape=jax.ShapeDtypeStruct((s, C_WIDTH), BF16),
        scratch_shapes=_flash_scratch(s, tq, tk, HEAD_DIM),
        compiler_params=_cparams(("parallel", "arbitrary")),
    )(z, z, z, z, cb)


def _diff_kernel(q_ref, k_ref, v_ref, g_ref, slope_ref, lam_ref, sg_ref, o_ref, kb_ref, *scratch,
                 tq, tk, lam_init):
    i = pl.program_id(1)
    slope = slope_ref[...]

    @pl.when(i == 0)
    def _():
        _fill_vt(v_ref, scratch[0], tk)
        r = lax.broadcasted_iota(jnp.int32, (tk, LANES), 0).astype(F32)
        lane = lax.broadcasted_iota(jnp.int32, (tk, LANES), 1)
        terms = _split_bf16(slope * r)
        kb = jnp.zeros((tk, LANES), F32)
        for n, t in enumerate(terms):
            kb = jnp.where(lane == n, t.astype(F32), kb)
        kb_ref[...] = kb.astype(BF16)

    def off_fn(j):
        return slope * (j * tk - i * tq).astype(F32)

    acc_ref = _pair_flash(q_ref, k_ref, lambda j0: kb_ref[...], off_fn, i, scratch,
                          tq=tq, tk=tk, v_rows=((0, LANES + ONES_ROWS), (0, LANES + ONES_ROWS)),
                          bias_rows=((0, N_SPLIT), (0, N_SPLIT)), bias_sign=1.0)
    lp = lam_ref[...]
    lam = (jnp.exp(jnp.sum(lp[0:1] * lp[1:2], axis=-1, keepdims=True))
           - jnp.exp(jnp.sum(lp[2:3] * lp[3:4], axis=-1, keepdims=True)) + lam_init)
    d = LANES
    o = (acc_ref[0, :d] / acc_ref[0, d:d + 1] - lam * (acc_ref[1, :d] / acc_ref[1, d:d + 1])).T
    o = o * lax.rsqrt(jnp.mean(o * o, axis=-1, keepdims=True) + RMS_EPS) * sg_ref[...]
    o = o * (1.0 - lam_init)
    o_ref[...] = (o * _silu(g_ref[...].astype(F32))).astype(o_ref.dtype)


def _diff_attention(z, slopes, lam_params, subln_g, lam_init, *, tq, tk):
    s = z.shape[0]
    qb, kb, vb, gb = 18, 26, 34, 42
    return pl.pallas_call(
        functools.partial(_diff_kernel, tq=tq, tk=tk, lam_init=lam_init),
        grid=(B_HEADS, s // tq),
        in_specs=[
            pl.BlockSpec((tq, LANES), lambda h, i: (i, qb + h)),
            pl.BlockSpec((s, LANES), lambda h, i: (0, kb + h)),
            pl.BlockSpec((s, LANES), lambda h, i: (0, vb + h)),
            pl.BlockSpec((tq, LANES), lambda h, i: (i, gb + h)),
            pl.BlockSpec((None, 1, 1), lambda h, i: (h, 0, 0)),
            pl.BlockSpec((4, HEAD_DIM), lambda h, i: (0, 0)),
            pl.BlockSpec((1, LANES), lambda h, i: (0, 0)),
        ],
        out_specs=pl.BlockSpec((tq, LANES), lambda h, i: (i, h)),
        out_shape=jax.ShapeDtypeStruct((s, B_WIDTH), BF16),
        scratch_shapes=[pltpu.VMEM((tk, LANES), BF16)] + _flash_scratch(s, tq, tk, LANES),
        compiler_params=_cparams(("parallel", "arbitrary")),
    )(z, z, z, z, slopes.reshape(B_HEADS, 1, 1), lam_params, subln_g.reshape(1, LANES))


def _swa_kernel(slope_ref, sink_ref, q_ref, kp_ref, kc_ref, vp_ref, vc_ref, g_ref, o_ref):
    i = pl.program_id(0)
    blk = WINDOW
    lane = lax.broadcasted_iota(jnp.int32, (1, LANES), 1)
    lo = lane < HEAD_DIM

    def swap_halves(a):
        return pltpu.roll(a.astype(F32), HEAD_DIM, 1).astype(BF16)

    k = jnp.concatenate([kp_ref[...], kc_ref[...]], axis=0)
    v = jnp.concatenate([vp_ref[...], vc_ref[...]], axis=0)
    k_sw, v_sw = swap_halves(k), swap_halves(v)

    r = lax.broadcasted_iota(jnp.int32, (blk, 2 * blk), 0)
    c = lax.broadcasted_iota(jnp.int32, (blk, 2 * blk), 1)
    dist = r - c + blk
    valid = (dist >= 0) & (dist < WINDOW) & ((c >= blk) | (i > 0))
    dist_f = dist.astype(F32)

    for pair in range(A_HEADS // 2):
        qp = q_ref[:, pair * LANES:(pair + 1) * LANES] * jnp.asarray(SCALE, BF16)
        zero = jnp.zeros_like(qp)
        outs = []
        for a in range(2):
            h = 2 * pair + a
            kv = h // A_GROUP
            qh = jnp.where(lo, qp, zero) if a == 0 else jnp.where(lo, zero, qp)
            kk, vv = (k, v) if a == kv else (k_sw, v_sw)
            s = lax.dot_general(qh, kk, (((1,), (1,)), ((), ())), preferred_element_type=F32)
            s = jnp.where(valid, s - slope_ref[h] * dist_f, NEG_INF)
            sink = sink_ref[h]
            m = jnp.maximum(jnp.max(s, axis=-1, keepdims=True), sink)
            e = jnp.exp(s - m)
            denom = jnp.sum(e, axis=-1, keepdims=True) + jnp.exp(sink - m)
            p = (e / denom).astype(BF16)
            outs.append(jnp.dot(p, vv, preferred_element_type=F32))
        y = jnp.where(lo, outs[0], outs[1])
        g = g_ref[:, pair * LANES:(pair + 1) * LANES].astype(F32)
        o_ref[:, pair * LANES:(pair + 1) * LANES] = (y * _silu(g)).astype(o_ref.dtype)


def _swa_attention(z, slopes, sinks):
    s = z.shape[0]
    blk = WINDOW
    nq = A_WIDTH // LANES
    kb, vb = nq, nq + 1
    g_col = (nq + 2) * LANES
    smem = pl.BlockSpec(memory_space=pltpu.SMEM)
    prev = lambda i: (jnp.maximum(i - 1, 0), kb)
    prev_v = lambda i: (jnp.maximum(i - 1, 0), vb)
    return pl.pallas_call(
        _swa_kernel,
        grid=(s // blk,),
        in_specs=[
            smem, smem,
            pl.BlockSpec((blk, A_WIDTH), lambda i: (i, 0)),
            pl.BlockSpec((blk, LANES), prev),
            pl.BlockSpec((blk, LANES), lambda i: (i, kb)),
            pl.BlockSpec((blk, LANES), prev_v),
            pl.BlockSpec((blk, LANES), lambda i: (i, vb)),
            pl.BlockSpec((blk, A_WIDTH), lambda i: (i, 0)),
        ],
        out_specs=pl.BlockSpec((blk, A_WIDTH), lambda i: (i, 0)),
        out_shape=jax.ShapeDtypeStruct((s, A_WIDTH), BF16),
        compiler_params=_cparams(("parallel",)),
    )(slopes, sinks, z, z, z, z, z, z[:, g_col:g_col + A_WIDTH])


def _out_kernel(*refs, n_y, final):
    x_ref = refs[0]
    y_refs = refs[1:1 + n_y]
    w_refs = refs[1 + n_y:1 + 2 * n_y]
    p_ref, wp_ref, gn_ref, wg_ref = refs[1 + 2 * n_y:5 + 2 * n_y]
    if final:
        fg_ref, o_ref = refs[5 + 2 * n_y:]
    else:
        (o_ref,) = refs[5 + 2 * n_y:]

    x1 = x_ref[...]
    for y_ref, w_ref in zip(y_refs, w_refs):
        x1 = x1 + jnp.dot(y_ref[...], w_ref[...], preferred_element_type=F32)
    hn = x1 * lax.rsqrt(jnp.mean(x1 * x1, axis=-1, keepdims=True) + RMS_EPS) * gn_ref[...]
    gate = jax.nn.sigmoid(jnp.dot(hn.astype(BF16), wg_ref[...], preferred_element_type=F32))
    pp = jnp.dot(p_ref[...].astype(BF16), wp_ref[...], preferred_element_type=F32)
    x2 = x1 + gate * pp
    if final:
        x2 = x2 * lax.rsqrt(jnp.mean(x2 * x2, axis=-1, keepdims=True) + RMS_EPS) * fg_ref[...]
    o_ref[...] = x2


def _out_block(x, ys, ws, p, wp, gn, wg, final_g=None, *, tm):
    s, d = x.shape
    n_y = len(ys)
    final = final_g is not None
    const = lambda i: (0, 0)
    single = pl.Buffered(1)
    in_specs = [pl.BlockSpec((tm, d), lambda i: (i, 0))]
    in_specs += [pl.BlockSpec((tm, y.shape[1]), lambda i: (i, 0)) for y in ys]
    in_specs += [pl.BlockSpec(w.shape, const, pipeline_mode=single) for w in ws]
    in_specs += [
        pl.BlockSpec((tm, p.shape[1]), lambda i: (i, 0)),
        pl.BlockSpec(wp.shape, const, pipeline_mode=single),
        pl.BlockSpec((1, d), const),
        pl.BlockSpec(wg.shape, const, pipeline_mode=single),
    ]
    args = [x, *ys, *ws, p, wp, gn.reshape(1, d), wg]
    if final:
        in_specs.append(pl.BlockSpec((1, d), const))
        args.append(final_g.reshape(1, d))
    return pl.pallas_call(
        functools.partial(_out_kernel, n_y=n_y, final=final),
        grid=(s // tm,),
        in_specs=in_specs,
        out_specs=pl.BlockSpec((tm, d), lambda i: (i, 0)),
        out_shape=jax.ShapeDtypeStruct((s, d), F32),
        compiler_params=_cparams(("parallel",)),
    )(*args)


def _alibi_slopes(n):
    return jnp.asarray([2.0 ** (-8.0 * (h + 1) / n) for h in range(n)], dtype=F32)


def kernel(x, p, norm_g, w_in_ab, w_out_ab, attn_sinks, diff_lambda, diff_subln_g,
           w_in_c, w_out_c, forget_bias, ple_proj, ple_gate, ple_norm_g, final_norm_g):
    b, s, d = x.shape
    assert (b, s, d) == (1, SEQ, D_MODEL)
    xs = x.reshape(s, d)
    for i in range(DEPTH):
        j = i // 2
        last = i == DEPTH - 1
        if i % 2 == 0:
            lam_init = 0.8 - 0.6 * math.exp(-0.3 * i)
            z = _norm_matmul(xs, norm_g[i], w_in_ab[j].astype(BF16), tm=512, tn=1280)
            ya = _swa_attention(z, _alibi_slopes(A_HEADS), attn_sinks[j].astype(F32))
            yb = _diff_attention(z, _alibi_slopes(B_HEADS), diff_lambda[j].astype(F32),
                                 diff_subln_g[j].astype(F32), lam_init, tq=1024, tk=512)
            w_out = w_out_ab[j].astype(BF16)
            ys, ws = [ya, yb], [w_out[:A_WIDTH], w_out[A_WIDTH:]]
        else:
            w = w_in_c[j]
            w_main = jnp.concatenate([w[:, :3 * C_WIDTH], w[:, 3 * C_WIDTH + C_HEADS:]], axis=1).astype(BF16)
            w_f = jnp.pad(w[:, 3 * C_WIDTH:3 * C_WIDTH + C_HEADS], ((0, 0), (0, LANES - C_HEADS))).astype(BF16)
            z, f = _norm_matmul(xs, norm_g[i], w_main, w_f, tm=512, tn=1024)
            f_bias = jnp.pad(forget_bias[j].astype(F32), (0, LANES - C_HEADS)).reshape(1, LANES)
            cb = _logsig_cumsum(f, f_bias, C_HEADS // 2)
            ys, ws = [_fox_attention(z, cb, tq=1024, tk=512)], [w_out_c[j].astype(BF16)]
        xs = _out_block(xs, ys, ws, p[i].reshape(s, D_PLE), ple_proj[i].astype(BF16),
                        ple_norm_g[i], ple_gate[i].astype(BF16),
                        final_norm_g if last else None, tm=256)
    return xs.reshape(b, s, d)
```

```python
import functools
import math

import jax
import jax.numpy as jnp
from jax import lax
from jax.experimental import pallas as pl
from jax.experimental.pallas import tpu as pltpu

F32 = jnp.float32
BF16 = jnp.bfloat16

D_MODEL = 2048
SEQ = 8192
DEPTH = 2
HEAD_DIM = 64
D_PLE = 256
WINDOW = 128
RMS_EPS = 1e-6
A_HEADS = 16
A_KV_HEADS = 2
A_GROUP = A_HEADS // A_KV_HEADS
A_WIDTH = A_HEADS * HEAD_DIM
B_HEADS = 8
B_WIDTH = B_HEADS * 2 * HEAD_DIM
C_HEADS = 32
C_WIDTH = C_HEADS * HEAD_DIM
AB_IN = 6400
SCALE = HEAD_DIM ** -0.5

LANES = 128
NEG_INF = float("-inf")
VMEM_LIMIT = 48 * 1024 * 1024


def _cparams(sem):
    return pltpu.CompilerParams(dimension_semantics=sem, vmem_limit_bytes=VMEM_LIMIT)


def _silu(g):
    return g * (1.0 / (1.0 + jnp.exp(-g)))


def _norm_matmul_kernel(x_ref, g_ref, w_ref, *rest, with_extra):
    if with_extra:
        we_ref, o_ref, oe_ref, h_ref = rest
    else:
        o_ref, h_ref = rest

    @pl.when(pl.program_id(1) == 0)
    def _():
        x = x_ref[...]
        ms = jnp.mean(x * x, axis=-1, keepdims=True)
        h = (x * lax.rsqrt(ms + RMS_EPS) * g_ref[...]).astype(BF16)
        h_ref[...] = h
        if with_extra:
            oe_ref[...] = jnp.dot(h, we_ref[...], preferred_element_type=F32)

    o_ref[...] = jnp.dot(h_ref[...], w_ref[...], preferred_element_type=F32).astype(o_ref.dtype)


def _norm_matmul(x, g, w, w_extra=None, *, tm, tn):
    s, d = x.shape
    n = w.shape[1]
    with_extra = w_extra is not None
    in_specs = [
        pl.BlockSpec((tm, d), lambda i, j: (i, 0)),
        pl.BlockSpec((1, d), lambda i, j: (0, 0)),
        pl.BlockSpec((d, tn), lambda i, j: (0, j)),
    ]
    out_shape = [jax.ShapeDtypeStruct((s, n), BF16)]
    out_specs = [pl.BlockSpec((tm, tn), lambda i, j: (i, j))]
    args = [x, g.reshape(1, d), w]
    if with_extra:
        ne = w_extra.shape[1]
        in_specs.append(pl.BlockSpec((d, ne), lambda i, j: (0, 0)))
        out_shape.append(jax.ShapeDtypeStruct((s, ne), F32))
        out_specs.append(pl.BlockSpec((tm, ne), lambda i, j: (i, 0)))
        args.append(w_extra)
    res = pl.pallas_call(
        functools.partial(_norm_matmul_kernel, with_extra=with_extra),
        grid=(s // tm, n // tn),
        in_specs=in_specs,
        out_specs=out_specs,
        out_shape=out_shape,
        scratch_shapes=[pltpu.VMEM((tm, d), BF16)],
        compiler_params=_cparams(("parallel", "arbitrary")),
    )(*args)
    return res if with_extra else res[0]


N_SPLIT = 3


def _split_bf16(x):
    parts = []
    for _ in range(N_SPLIT):
        t = x.astype(BF16)
        parts.append(t)
        x = x - t.astype(F32)
    return parts


def _logsig_cumsum_kernel(f_ref, b_ref, o_ref, carry_ref, sel_ref, *, blk):
    n = f_ref.shape[1]

    @pl.when(pl.program_id(0) == 0)
    def _():
        carry_ref[...] = jnp.zeros_like(carry_ref)
        r = lax.broadcasted_iota(jnp.int32, sel_ref.shape, 0)
        l = lax.broadcasted_iota(jnp.int32, sel_ref.shape, 1)
        part, head = r // n, r % n
        lane = jnp.where(l // LANES == head // 2, l % LANES, -1)
        sel_ref[...] = jnp.where(lane == N_SPLIT * (head % 2) + part, 1.0, 0.0).astype(BF16)

    z = f_ref[...] + b_ref[...]
    ls = jnp.minimum(z, 0.0) - jnp.log1p(jnp.exp(-jnp.abs(z)))
    row = lax.broadcasted_iota(jnp.int32, (blk, blk), 0)
    col = lax.broadcasted_iota(jnp.int32, (blk, blk), 1)
    lower = (col <= row).astype(F32)
    c = jnp.dot(lower, ls, preferred_element_type=F32,
                precision=lax.Precision.HIGHEST) + carry_ref[...]
    carry_ref[...] = c[blk - 1:blk, :]
    terms = jnp.concatenate(_split_bf16(c), axis=1)
    o_ref[...] = jnp.dot(terms, sel_ref[...], preferred_element_type=F32).astype(BF16)


def _logsig_cumsum(f, bias_row, n_pairs, *, blk=256):
    s, n = f.shape
    return pl.pallas_call(
        functools.partial(_logsig_cumsum_kernel, blk=blk),
        grid=(s // blk,),
        in_specs=[pl.BlockSpec((blk, n), lambda i: (i, 0)),
                  pl.BlockSpec((1, n), lambda i: (0, 0))],
        out_specs=pl.BlockSpec((blk, n_pairs * LANES), lambda i: (i, 0)),
        out_shape=jax.ShapeDtypeStruct((s, n_pairs * LANES), BF16),
        scratch_shapes=[pltpu.VMEM((1, n), F32), pltpu.VMEM((N_SPLIT * n, n_pairs * LANES), BF16)],
        compiler_params=_cparams(("arbitrary",)),
    )(f, bias_row)


ONES_ROWS = 16
CHUNK = 256
LAG_LIMIT = 30.0


def _flash_scratch(s, tq, tk, rows):
    return [
        pltpu.VMEM((s // tk, LANES + ONES_ROWS, tk), BF16),
        pltpu.VMEM((2, 2 * LANES, tq), BF16),
        pltpu.VMEM((2, 1, tq), F32),
        pltpu.VMEM((2, rows + ONES_ROWS, tq), F32),
        pltpu.VMEM((2, 1, tq), F32),
        pltpu.VMEM((2, 1, tq), F32),
        *[pltpu.VMEM((tk, tq), F32) for _ in range(4)],
        *[pltpu.VMEM((1, tq), F32) for _ in range(4)],
        *[pltpu.VMEM((tk, tq), BF16) for _ in range(4)],
        *[pltpu.VMEM((1, tq), F32) for _ in range(4)],
    ]


def _fill_vt(v_ref, vt_ref, tk):
    def body(c, carry):
        vt_ref[c, :LANES, :] = v_ref[pl.ds(pl.multiple_of(c * tk, tk), tk), :].T
        vt_ref[c, LANES:, :] = jnp.ones((ONES_ROWS, tk), BF16)
        return carry
    lax.fori_loop(0, vt_ref.shape[0], body, 0)


def _pair_flash(q_ref, k_ref, kb_fn, off_fn, i, scratch, *, tq, tk, v_rows, bias_rows, bias_sign):
    assert tq == 2 * tk
    vt_ref, qt_ref, m_ref, acc_ref, g_ref, bad_ref = scratch[:6]
    s_ref, mt_ref, p_ref, a_ref = ({(slot, h): scratch[6 + 4 * n + 2 * slot + h]
                                    for slot in range(2) for h in range(2)} for n in range(4))
    row = lax.broadcasted_iota(jnp.int32, (LANES, 1), 0)
    lo = row < HEAD_DIM
    qt = (q_ref[...] * jnp.asarray(SCALE, BF16)).T
    zero = jnp.zeros_like(qt)
    for h in range(2):
        qt_ref[h, :LANES, :] = jnp.where(lo, qt, zero) if h == 0 else jnp.where(lo, zero, qt)
        b0, b1 = bias_rows[h]
        qt_ref[h, LANES:, :] = jnp.where((row >= b0) & (row < b1), bias_sign, 0.0).astype(BF16) + zero

    lanes = [(h, slice(c, c + CHUNK)) for c in range(0, tq, CHUNK) for h in range(2)]

    def tile_of(n):
        return jnp.where(n < 2, 2 * i + n, n - 2)

    def raw_scores(j, diag, h, cs):
        j0 = pl.multiple_of(j * tk, tk)
        lhs = jnp.concatenate([k_ref[pl.ds(j0, tk), :], kb_fn(j0)], axis=1)
        t = jnp.dot(lhs, qt_ref[h, :, cs], preferred_element_type=F32)
        if diag is not None:
            kpos = lax.broadcasted_iota(jnp.int32, t.shape, 0) + diag * tk
            qpos = lax.broadcasted_iota(jnp.int32, t.shape, 1) + cs.start
            t = jnp.where(kpos <= qpos, t, NEG_INF)
        return t

    def scores(j, slot, diag, h, cs):
        t = raw_scores(j, diag, h, cs)
        s_ref[slot, h][:, cs] = t
        mt_ref[slot, h][:, cs] = jnp.max(t, axis=0, keepdims=True)

    def values(j, slot, h, cs, first=False):
        r0, r1 = v_rows[h]
        vt = vt_ref[j, r0:r1, :]
        if r1 != LANES + ONES_ROWS:
            vt = jnp.concatenate([vt, vt_ref[j, LANES:, :]], axis=0)
        pv = jnp.dot(vt, p_ref[slot, h][:, cs], preferred_element_type=F32)
        acc_ref[h, :, cs] = pv if first else acc_ref[h, :, cs] * a_ref[slot, h][:, cs] + pv

    def softmax(j, slot, h, cs, first=False):
        off = off_fn(j)
        mt = mt_ref[slot, h][:, cs]
        mt = mt if off is None else mt + off
        if first:
            m_new = mt
        else:
            m_old = m_ref[h, :, cs]
            m_new = jnp.maximum(m_old, mt)
            a_ref[1 - slot, h][:, cs] = jnp.exp(m_old - m_new)
        m_ref[h, :, cs] = m_new
        shift = m_new if off is None else m_new - off
        p_ref[1 - slot, h][:, cs] = jnp.exp(s_ref[slot, h][:, cs] - shift).astype(BF16)

    def exact_step(t, slot, diag=None, first=False, last=False):
        for h, cs in lanes:
            if not first:
                values(tile_of(t - 1), slot, h, cs)
            softmax(tile_of(t), slot, h, cs)
            if not last:
                scores(tile_of(t + 1), 1 - slot, diag, h, cs)

    def exact_sweep():
        for h in range(2):
            m_ref[h] = jnp.full((1, tq), NEG_INF, F32)
            acc_ref[h] = jnp.zeros(acc_ref.shape[1:], F32)
        for h, cs in lanes:
            scores(2 * i, 0, 0, h, cs)
        exact_step(0, 0, 1, first=True)

        def two_steps(u, carry):
            exact_step(2 * u + 1, 1)
            exact_step(2 * u + 2, 0)
            return carry

        lax.fori_loop(0, i, two_steps, 0)
        exact_step(2 * i + 1, 1, last=True)
        for h, cs in lanes:
            values(tile_of(2 * i + 1), 0, h, cs)

    def fused(n, slot, diag, h, cs):
        j = tile_of(n)
        off = off_fn(j)
        t = raw_scores(j, diag, h, cs)
        ref = m_ref[h, :, cs]
        p_ref[slot, h][:, cs] = jnp.exp(t - (ref if off is None else ref - off)).astype(BF16)
        mt = jnp.max(t, axis=0, keepdims=True)
        mt = mt if off is None else mt + off
        a_ref[slot, h][:, cs] = g_ref[h, :, cs]
        m_new = jnp.maximum(ref, mt)
        g_ref[h, :, cs] = jnp.exp(ref - m_new)
        m_ref[h, :, cs] = m_new
        bad_ref[h, :, cs] = jnp.maximum(bad_ref[h, :, cs], mt - ref)

    def lagged_step(n, slot, diag=None, first=False):
        for h, cs in lanes:
            values(tile_of(n - 1), 1 - slot, h, cs, first=first)
            fused(n, slot, diag, h, cs)

    def lagged_sweep():
        for h, cs in lanes:
            scores(2 * i, 0, 0, h, cs)
        for h, cs in lanes:
            softmax(2 * i, 0, h, cs, first=True)
            g_ref[h, :, cs] = jnp.ones((1, CHUNK), F32)
            bad_ref[h, :, cs] = jnp.full((1, CHUNK), NEG_INF, F32)
        lagged_step(1, 0, 1, first=True)

        def two_steps(u, carry):
            lagged_step(2 * u + 2, 1)
            lagged_step(2 * u + 3, 0)
            return carry

        lax.fori_loop(0, i, two_steps, 0)
        for h, cs in lanes:
            values(tile_of(2 * i + 1), 0, h, cs)

    lagged_sweep()
    worst = jnp.max(bad_ref[...])

    @pl.when(jnp.logical_not(worst <= LAG_LIMIT))
    def _():
        exact_sweep()

    return acc_ref


def _fox_kernel(q_ref, k_ref, v_ref, g_ref, cb_ref, o_ref, *scratch, tq, tk):
    i = pl.program_id(1)

    @pl.when(i == 0)
    def _():
        _fill_vt(v_ref, scratch[0], tk)

    acc_ref = _pair_flash(q_ref, k_ref, lambda j0: cb_ref[pl.ds(j0, tk), :], lambda j: None, i, scratch,
                          tq=tq, tk=tk, v_rows=((0, HEAD_DIM), (HEAD_DIM, LANES)),
                          bias_rows=((0, N_SPLIT), (N_SPLIT, 2 * N_SPLIT)), bias_sign=-1.0)
    d = HEAD_DIM
    y = jnp.concatenate([acc_ref[0, :d] / acc_ref[0, d:d + 1], acc_ref[1, :d] / acc_ref[1, d:d + 1]], axis=0).T
    o_ref[...] = (y * _silu(g_ref[...].astype(F32))).astype(o_ref.dtype)


def _fox_attention(z, cb, *, tq, tk):
    s = z.shape[0]
    nb = C_WIDTH // LANES
    return pl.pallas_call(
        functools.partial(_fox_kernel, tq=tq, tk=tk),
        grid=(nb, s // tq),
        in_specs=[
            pl.BlockSpec((tq, LANES), lambda p, i: (i, p)),
            pl.BlockSpec((s, LANES), lambda p, i: (0, nb + p)),
            pl.BlockSpec((s, LANES), lambda p, i: (0, 2 * nb + p)),
            pl.BlockSpec((tq, LANES), lambda p, i: (i, 3 * nb + p)),
            pl.BlockSpec((s, LANES), lambda p, i: (0, p)),
        ],
        out_specs=pl.BlockSpec((tq, LANES), lambda p, i: (i, p)),
        out_shape=jax.ShapeDtypeStruct((s, C_WIDTH), BF16),
        scratch_shapes=_flash_scratch(s, tq, tk, HEAD_DIM),
        compiler_params=_cparams(("parallel", "arbitrary")),
    )(z, z, z, z, cb)


def _diff_kernel(q_ref, k_ref, v_ref, g_ref, slope_ref, lam_ref, sg_ref, o_ref, kb_ref, *scratch,
                 tq, tk, lam_init):
    i = pl.program_id(1)
    slope = slope_ref[...]

    @pl.when(i == 0)
    def _():
        _fill_vt(v_ref, scratch[0], tk)
        r = lax.broadcasted_iota(jnp.int32, (tk, LANES), 0).astype(F32)
        lane = lax.broadcasted_iota(jnp.int32, (tk, LANES), 1)
        terms = _split_bf16(slope * r)
        kb = jnp.zeros((tk, LANES), F32)
        for n, t in enumerate(terms):
            kb = jnp.where(lane == n, t.astype(F32), kb)
        kb_ref[...] = kb.astype(BF16)

    def off_fn(j):
        return slope * (j * tk - i * tq).astype(F32)

    acc_ref = _pair_flash(q_ref, k_ref, lambda j0: kb_ref[...], off_fn, i, scratch,
                          tq=tq, tk=tk, v_rows=((0, LANES + ONES_ROWS), (0, LANES + ONES_ROWS)),
                          bias_rows=((0, N_SPLIT), (0, N_SPLIT)), bias_sign=1.0)
    lp = lam_ref[...]
    lam = (jnp.exp(jnp.sum(lp[0:1] * lp[1:2], axis=-1, keepdims=True))
           - jnp.exp(jnp.sum(lp[2:3] * lp[3:4], axis=-1, keepdims=True)) + lam_init)
    d = LANES
    o = (acc_ref[0, :d] / acc_ref[0, d:d + 1] - lam * (acc_ref[1, :d] / acc_ref[1, d:d + 1])).T
    o = o * lax.rsqrt(jnp.mean(o * o, axis=-1, keepdims=True) + RMS_EPS) * sg_ref[...]
    o = o * (1.0 - lam_init)
    o_ref[...] = (o * _silu(g_ref[...].astype(F32))).astype(o_ref.dtype)


def _diff_attention(z, slopes, lam_params, subln_g, lam_init, *, tq, tk):
    s = z.shape[0]
    qb, kb, vb, gb = 18, 26, 34, 42
    return pl.pallas_call(
        functools.partial(_diff_kernel, tq=tq, tk=tk, lam_init=lam_init),
        grid=(B_HEADS, s // tq),
        in_specs=[
            pl.BlockSpec((tq, LANES), lambda h, i: (i, qb + h)),
            pl.BlockSpec((s, LANES), lambda h, i: (0, kb + h)),
            pl.BlockSpec((s, LANES), lambda h, i: (0, vb + h)),
            pl.BlockSpec((tq, LANES), lambda h, i: (i, gb + h)),
            pl.BlockSpec((None, 1, 1), lambda h, i: (h, 0, 0)),
            pl.BlockSpec((4, HEAD_DIM), lambda h, i: (0, 0)),
            pl.BlockSpec((1, LANES), lambda h, i: (0, 0)),
        ],
        out_specs=pl.BlockSpec((tq, LANES), lambda h, i: (i, h)),
        out_shape=jax.ShapeDtypeStruct((s, B_WIDTH), BF16),
        scratch_shapes=[pltpu.VMEM((tk, LANES), BF16)] + _flash_scratch(s, tq, tk, LANES),
        compiler_params=_cparams(("parallel", "arbitrary")),
    )(z, z, z, z, slopes.reshape(B_HEADS, 1, 1), lam_params, subln_g.reshape(1, LANES))


def _swa_kernel(slope_ref, sink_ref, q_ref, kp_ref, kc_ref, vp_ref, vc_ref, g_ref, o_ref):
    i = pl.program_id(0)
    blk = WINDOW
    lane = lax.broadcasted_iota(jnp.int32, (1, LANES), 1)
    lo = lane < HEAD_DIM

    def swap_halves(a):
        return pltpu.roll(a.astype(F32), HEAD_DIM, 1).astype(BF16)

    k = jnp.concatenate([kp_ref[...], kc_ref[...]], axis=0)
    v = jnp.concatenate([vp_ref[...], vc_ref[...]], axis=0)
    k_sw, v_sw = swap_halves(k), swap_halves(v)

    r = lax.broadcasted_iota(jnp.int32, (blk, 2 * blk), 0)
    c = lax.broadcasted_iota(jnp.int32, (blk, 2 * blk), 1)
    dist = r - c + blk
    valid = (dist >= 0) & (dist < WINDOW) & ((c >= blk) | (i > 0))
    dist_f = dist.astype(F32)

    for pair in range(A_HEADS // 2):
        qp = q_ref[:, pair * LANES:(pair + 1) * LANES] * jnp.asarray(SCALE, BF16)
        zero = jnp.zeros_like(qp)
        outs = []
        for a in range(2):
            h = 2 * pair + a
            kv = h // A_GROUP
            qh = jnp.where(lo, qp, zero) if a == 0 else jnp.where(lo, zero, qp)
            kk, vv = (k, v) if a == kv else (k_sw, v_sw)
            s = lax.dot_general(qh, kk, (((1,), (1,)), ((), ())), preferred_element_type=F32)
            s = jnp.where(valid, s - slope_ref[h] * dist_f, NEG_INF)
            sink = sink_ref[h]
            m = jnp.maximum(jnp.max(s, axis=-1, keepdims=True), sink)
            e = jnp.exp(s - m)
            denom = jnp.sum(e, axis=-1, keepdims=True) + jnp.exp(sink - m)
            p = (e / denom).astype(BF16)
            outs.append(jnp.dot(p, vv, preferred_element_type=F32))
        y = jnp.where(lo, outs[0], outs[1])
        g = g_ref[:, pair * LANES:(pair + 1) * LANES].astype(F32)
        o_ref[:, pair * LANES:(pair + 1) * LANES] = (y * _silu(g)).astype(o_ref.dtype)


def _swa_attention(z, slopes, sinks):
    s = z.shape[0]
    blk = WINDOW
    nq = A_WIDTH // LANES
    kb, vb = nq, nq + 1
    g_col = (nq + 2) * LANES
    smem = pl.BlockSpec(memory_space=pltpu.SMEM)
    prev = lambda i: (jnp.maximum(i - 1, 0), kb)
    prev_v = lambda i: (jnp.maximum(i - 1, 0), vb)
    return pl.pallas_call(
        _swa_kernel,
        grid=(s // blk,),
        in_specs=[
            smem, smem,
            pl.BlockSpec((blk, A_WIDTH), lambda i: (i, 0)),
            pl.BlockSpec((blk, LANES), prev),
            pl.BlockSpec((blk, LANES), lambda i: (i, kb)),
            pl.BlockSpec((blk, LANES), prev_v),
            pl.BlockSpec((blk, LANES), lambda i: (i, vb)),
            pl.BlockSpec((blk, A_WIDTH), lambda i: (i, 0)),
        ],
        out_specs=pl.BlockSpec((blk, A_WIDTH), lambda i: (i, 0)),
        out_shape=jax.ShapeDtypeStruct((s, A_WIDTH), BF16),
        compiler_params=_cparams(("parallel",)),
    )(slopes, sinks, z, z, z, z, z, z[:, g_col:g_col + A_WIDTH])


def _out_kernel(*refs, n_y, final):
    x_ref = refs[0]
    y_refs = refs[1:1 + n_y]
    w_refs = refs[1 + n_y:1 + 2 * n_y]
    p_ref, wp_ref, gn_ref, wg_ref = refs[1 + 2 * n_y:5 + 2 * n_y]
    if final:
        fg_ref, o_ref = refs[5 + 2 * n_y:]
    else:
        (o_ref,) = refs[5 + 2 * n_y:]

    x1 = x_ref[...]
    for y_ref, w_ref in zip(y_refs, w_refs):
        x1 = x1 + jnp.dot(y_ref[...], w_ref[...], preferred_element_type=F32)
    hn = x1 * lax.rsqrt(jnp.mean(x1 * x1, axis=-1, keepdims=True) + RMS_EPS) * gn_ref[...]
    gate = jax.nn.sigmoid(jnp.dot(hn.astype(BF16), wg_ref[...], preferred_element_type=F32))
    pp = jnp.dot(p_ref[...].astype(BF16), wp_ref[...], preferred_element_type=F32)
    x2 = x1 + gate * pp
    if final:
        x2 = x2 * lax.rsqrt(jnp.mean(x2 * x2, axis=-1, keepdims=True) + RMS_EPS) * fg_ref[...]
    o_ref[...] = x2


def _out_block(x, ys, ws, p, wp, gn, wg, final_g=None, *, tm):
    s, d = x.shape
    n_y = len(ys)
    final = final_g is not None
    const = lambda i: (0, 0)
    single = pl.Buffered(1)
    in_specs = [pl.BlockSpec((tm, d), lambda i: (i, 0))]
    in_specs += [pl.BlockSpec((tm, y.shape[1]), lambda i: (i, 0)) for y in ys]
    in_specs += [pl.BlockSpec(w.shape, const, pipeline_mode=single) for w in ws]
    in_specs += [
        pl.BlockSpec((tm, p.shape[1]), lambda i: (i, 0)),
        pl.BlockSpec(wp.shape, const, pipeline_mode=single),
        pl.BlockSpec((1, d), const),
        pl.BlockSpec(wg.shape, const, pipeline_mode=single),
    ]
    args = [x, *ys, *ws, p, wp, gn.reshape(1, d), wg]
    if final:
        in_specs.append(pl.BlockSpec((1, d), const))
        args.append(final_g.reshape(1, d))
    return pl.pallas_call(
        functools.partial(_out_kernel, n_y=n_y, final=final),
        grid=(s // tm,),
        in_specs=in_specs,
        out_specs=pl.BlockSpec((tm, d), lambda i: (i, 0)),
        out_shape=jax.ShapeDtypeStruct((s, d), F32),
        compiler_params=_cparams(("parallel",)),
    )(*args)


def _alibi_slopes(n):
    return jnp.asarray([2.0 ** (-8.0 * (h + 1) / n) for h in range(n)], dtype=F32)


def kernel(x, p, norm_g, w_in_ab, w_out_ab, attn_sinks, diff_lambda, diff_subln_g,
           w_in_c, w_out_c, forget_bias, ple_proj, ple_gate, ple_norm_g, final_norm_g):
    b, s, d = x.shape
    assert (b, s, d) == (1, SEQ, D_MODEL)
    xs = x.reshape(s, d)
    for i in range(DEPTH):
        j = i // 2
        last = i == DEPTH - 1
        if i % 2 == 0:
            lam_init = 0.8 - 0.6 * math.exp(-0.3 * i)
            z = _norm_matmul(xs, norm_g[i], w_in_ab[j].astype(BF16), tm=512, tn=1280)
            ya = _swa_attention(z, _alibi_slopes(A_HEADS), attn_sinks[j].astype(F32))
            yb = _diff_attention(z, _alibi_slopes(B_HEADS), diff_lambda[j].astype(F32),
                                 diff_subln_g[j].astype(F32), lam_init, tq=1024, tk=512)
            w_out = w_out_ab[j].astype(BF16)
            ys, ws = [ya, yb], [w_out[:A_WIDTH], w_out[A_WIDTH:]]
        else:
            w = w_in_c[j]
            w_main = jnp.concatenate([w[:, :3 * C_WIDTH], w[:, 3 * C_WIDTH + C_HEADS:]], axis=1).astype(BF16)
            w_f = jnp.pad(w[:, 3 * C_WIDTH:3 * C_WIDTH + C_HEADS], ((0, 0), (0, LANES - C_HEADS))).astype(BF16)
            z, f = _norm_matmul(xs, norm_g[i], w_main, w_f, tm=512, tn=1024)
            f_bias = jnp.pad(forget_bias[j].astype(F32), (0, LANES - C_HEADS)).reshape(1, LANES)
            cb = _logsig_cumsum(f, f_bias, C_HEADS // 2)
            ys, ws = [_fox_attention(z, cb, tq=1024, tk=512)], [w_out_c[j].astype(BF16)]
        xs = _out_block(xs, ys, ws, p[i].reshape(s, D_PLE), ple_proj[i].astype(BF16),
                        ple_norm_g[i], ple_gate[i].astype(BF16),
                        final_norm_g if last else None, tm=256)
    return xs.reshape(b, s, d)
```

```python
import functools
import math

import jax
import jax.numpy as jnp
from jax import lax
from jax.experimental import pallas as pl
from jax.experimental.pallas import tpu as pltpu

F32 = jnp.float32
BF16 = jnp.bfloat16

D_MODEL = 2048
SEQ = 8192
DEPTH = 2
HEAD_DIM = 64
D_PLE = 256
WINDOW = 128
RMS_EPS = 1e-6
A_HEADS = 16
A_KV_HEADS = 2
A_GROUP = A_HEADS // A_KV_HEADS
A_WIDTH = A_HEADS * HEAD_DIM
B_HEADS = 8
B_WIDTH = B_HEADS * 2 * HEAD_DIM
C_HEADS = 32
C_WIDTH = C_HEADS * HEAD_DIM
AB_IN = 6400
SCALE = HEAD_DIM ** -0.5

LANES = 128
NEG_INF = float("-inf")
VMEM_LIMIT = 48 * 1024 * 1024


def _cparams(sem):
    return pltpu.CompilerParams(dimension_semantics=sem, vmem_limit_bytes=VMEM_LIMIT)


def _silu(g):
    return g * (1.0 / (1.0 + jnp.exp(-g)))


def _norm_matmul_kernel(x_ref, g_ref, w_ref, *rest, with_extra):
    if with_extra:
        we_ref, o_ref, oe_ref, h_ref = rest
    else:
        o_ref, h_ref = rest

    @pl.when(pl.program_id(1) == 0)
    def _():
        x = x_ref[...]
        ms = jnp.mean(x * x, axis=-1, keepdims=True)
        h = (x * lax.rsqrt(ms + RMS_EPS) * g_ref[...]).astype(BF16)
        h_ref[...] = h
        if with_extra:
            oe_ref[...] = jnp.dot(h, we_ref[...], preferred_element_type=F32)

    o_ref[...] = jnp.dot(h_ref[...], w_ref[...], preferred_element_type=F32).astype(o_ref.dtype)


def _norm_matmul(x, g, w, w_extra=None, *, tm, tn):
    s, d = x.shape
    n = w.shape[1]
    with_extra = w_extra is not None
    in_specs = [
        pl.BlockSpec((tm, d), lambda i, j: (i, 0)),
        pl.BlockSpec((1, d), lambda i, j: (0, 0)),
        pl.BlockSpec((d, tn), lambda i, j: (0, j)),
    ]
    out_shape = [jax.ShapeDtypeStruct((s, n), BF16)]
    out_specs = [pl.BlockSpec((tm, tn), lambda i, j: (i, j))]
    args = [x, g.reshape(1, d), w]
    if with_extra:
        ne = w_extra.shape[1]
        in_specs.append(pl.BlockSpec((d, ne), lambda i, j: (0, 0)))
        out_shape.append(jax.ShapeDtypeStruct((s, ne), F32))
        out_specs.append(pl.BlockSpec((tm, ne), lambda i, j: (i, 0)))
        args.append(w_extra)
    res = pl.pallas_call(
        functools.partial(_norm_matmul_kernel, with_extra=with_extra),
        grid=(s // tm, n // tn),
        in_specs=in_specs,
        out_specs=out_specs,
        out_shape=out_shape,
        scratch_shapes=[pltpu.VMEM((tm, d), BF16)],
        compiler_params=_cparams(("parallel", "arbitrary")),
    )(*args)
    return res if with_extra else res[0]


N_SPLIT = 3


def _split_bf16(x):
    parts = []
    for _ in range(N_SPLIT):
        t = x.astype(BF16)
        parts.append(t)
        x = x - t.astype(F32)
    return parts


def _logsig_cumsum_kernel(f_ref, b_ref, o_ref, carry_ref, sel_ref, *, blk):
    n = f_ref.shape[1]

    @pl.when(pl.program_id(0) == 0)
    def _():
        carry_ref[...] = jnp.zeros_like(carry_ref)
        r = lax.broadcasted_iota(jnp.int32, sel_ref.shape, 0)
        l = lax.broadcasted_iota(jnp.int32, sel_ref.shape, 1)
        part, head = r // n, r % n
        lane = jnp.where(l // LANES == head // 2, l % LANES, -1)
        sel_ref[...] = jnp.where(lane == N_SPLIT * (head % 2) + part, 1.0, 0.0).astype(BF16)

    z = f_ref[...] + b_ref[...]
    ls = jnp.minimum(z, 0.0) - jnp.log1p(jnp.exp(-jnp.abs(z)))
    row = lax.broadcasted_iota(jnp.int32, (blk, blk), 0)
    col = lax.broadcasted_iota(jnp.int32, (blk, blk), 1)
    lower = (col <= row).astype(F32)
    c = jnp.dot(lower, ls, preferred_element_type=F32,
                precision=lax.Precision.HIGHEST) + carry_ref[...]
    carry_ref[...] = c[blk - 1:blk, :]
    terms = jnp.concatenate(_split_bf16(c), axis=1)
    o_ref[...] = jnp.dot(terms, sel_ref[...], preferred_element_type=F32).astype(BF16)


def _logsig_cumsum(f, bias_row, n_pairs, *, blk=256):
    s, n = f.shape
    return pl.pallas_call(
        functools.partial(_logsig_cumsum_kernel, blk=blk),
        grid=(s // blk,),
        in_specs=[pl.BlockSpec((blk, n), lambda i: (i, 0)),
                  pl.BlockSpec((1, n), lambda i: (0, 0))],
        out_specs=pl.BlockSpec((blk, n_pairs * LANES), lambda i: (i, 0)),
        out_shape=jax.ShapeDtypeStruct((s, n_pairs * LANES), BF16),
        scratch_shapes=[pltpu.VMEM((1, n), F32), pltpu.VMEM((N_SPLIT * n, n_pairs * LANES), BF16)],
        compiler_params=_cparams(("arbitrary",)),
    )(f, bias_row)


ONES_ROWS = 16
CHUNK = 256
LAG_LIMIT = 30.0


def _flash_scratch(s, tq, tk, rows):
    return [
        pltpu.VMEM((s // tk, LANES + ONES_ROWS, tk), BF16),
        pltpu.VMEM((2, 2 * LANES, tq), BF16),
        pltpu.VMEM((2, 1, tq), F32),
        pltpu.VMEM((2, rows + ONES_ROWS, tq), F32),
        pltpu.VMEM((2, 1, tq), F32),
        pltpu.VMEM((2, 1, tq), F32),
        *[pltpu.VMEM((tk, tq), F32) for _ in range(4)],
        *[pltpu.VMEM((1, tq), F32) for _ in range(4)],
        *[pltpu.VMEM((tk, tq), BF16) for _ in range(4)],
        *[pltpu.VMEM((1, tq), F32) for _ in range(4)],
    ]


def _fill_vt(v_ref, vt_ref, tk):
    def body(c, carry):
        vt_ref[c, :LANES, :] = v_ref[pl.ds(pl.multiple_of(c * tk, tk), tk), :].T
        vt_ref[c, LANES:, :] = jnp.ones((ONES_ROWS, tk), BF16)
        return carry
    lax.fori_loop(0, vt_ref.shape[0], body, 0)


def _pair_flash(q_ref, k_ref, kb_fn, off_fn, i, scratch, *, tq, tk, v_rows, bias_rows, bias_sign):
    assert tq == 2 * tk
    vt_ref, qt_ref, m_ref, acc_ref, g_ref, bad_ref = scratch[:6]
    s_ref, mt_ref, p_ref, a_ref = ({(slot, h): scratch[6 + 4 * n + 2 * slot + h]
                                    for slot in range(2) for h in range(2)} for n in range(4))
    row = lax.broadcasted_iota(jnp.int32, (LANES, 1), 0)
    lo = row < HEAD_DIM
    qt = (q_ref[...] * jnp.asarray(SCALE, BF16)).T
    zero = jnp.zeros_like(qt)
    for h in range(2):
        qt_ref[h, :LANES, :] = jnp.where(lo, qt, zero) if h == 0 else jnp.where(lo, zero, qt)
        b0, b1 = bias_rows[h]
        qt_ref[h, LANES:, :] = jnp.where((row >= b0) & (row < b1), bias_sign, 0.0).astype(BF16) + zero

    lanes = [(h, slice(c, c + CHUNK)) for c in range(0, tq, CHUNK) for h in range(2)]

    def tile_of(n):
        return jnp.where(n < 2, 2 * i + n, n - 2)

    def raw_scores(j, diag, h, cs):
        j0 = pl.multiple_of(j * tk, tk)
        lhs = jnp.concatenate([k_ref[pl.ds(j0, tk), :], kb_fn(j0)], axis=1)
        t = jnp.dot(lhs, qt_ref[h, :, cs], preferred_element_type=F32)
        if diag is not None:
            kpos = lax.broadcasted_iota(jnp.int32, t.shape, 0) + diag * tk
            qpos = lax.broadcasted_iota(jnp.int32, t.shape, 1) + cs.start
            t = jnp.where(kpos <= qpos, t, NEG_INF)
        return t

    def scores(j, slot, diag, h, cs):
        t = raw_scores(j, diag, h, cs)
        s_ref[slot, h][:, cs] = t
        mt_ref[slot, h][:, cs] = jnp.max(t, axis=0, keepdims=True)

    def values(j, slot, h, cs, first=False):
        r0, r1 = v_rows[h]
        vt = vt_ref[j, r0:r1, :]
        if r1 != LANES + ONES_ROWS:
            vt = jnp.concatenate([vt, vt_ref[j, LANES:, :]], axis=0)
        pv = jnp.dot(vt, p_ref[slot, h][:, cs], preferred_element_type=F32)
        acc_ref[h, :, cs] = pv if first else acc_ref[h, :, cs] * a_ref[slot, h][:, cs] + pv

    def softmax(j, slot, h, cs):
        off = off_fn(j)
        mt = mt_ref[slot, h][:, cs]
        m_old = m_ref[h, :, cs]
        m_new = jnp.maximum(m_old, mt if off is None else mt + off)
        a_ref[1 - slot, h][:, cs] = jnp.exp(m_old - m_new)
        m_ref[h, :, cs] = m_new
        shift = m_new if off is None else m_new - off
        p_ref[1 - slot, h][:, cs] = jnp.exp(s_ref[slot, h][:, cs] - shift).astype(BF16)

    def exact_step(t, slot, diag=None, first=False, last=False):
        for h, cs in lanes:
            if not first:
                values(tile_of(t - 1), slot, h, cs)
            softmax(tile_of(t), slot, h, cs)
            if not last:
                scores(tile_of(t + 1), 1 - slot, diag, h, cs)

    def exact_sweep():
        for h in range(2):
            m_ref[h] = jnp.full((1, tq), NEG_INF, F32)
            acc_ref[h] = jnp.zeros(acc_ref.shape[1:], F32)
        for h, cs in lanes:
            scores(2 * i, 0, 0, h, cs)
        exact_step(0, 0, 1, first=True)

        def two_steps(u, carry):
            exact_step(2 * u + 1, 1)
            exact_step(2 * u + 2, 0)
            return carry

        lax.fori_loop(0, i, two_steps, 0)
        exact_step(2 * i + 1, 1, last=True)
        for h, cs in lanes:
            values(tile_of(2 * i + 1), 0, h, cs)

    def fused(n, slot, diag, h, cs):
        j = tile_of(n)
        off = off_fn(j)
        t = raw_scores(j, diag, h, cs)
        ref = m_ref[h, :, cs]
        p_ref[slot, h][:, cs] = jnp.exp(t - (ref if off is None else ref - off)).astype(BF16)
        mt = jnp.max(t, axis=0, keepdims=True)
        mt = mt if off is None else mt + off
        a_ref[slot, h][:, cs] = g_ref[h, :, cs]
        m_new = jnp.maximum(ref, mt)
        g_ref[h, :, cs] = jnp.exp(ref - m_new)
        m_ref[h, :, cs] = m_new
        bad_ref[h, :, cs] = jnp.maximum(bad_ref[h, :, cs], mt - ref)

    def lagged_step(n, slot, diag=None, first=False):
        for h, cs in lanes:
            values(tile_of(n - 1), 1 - slot, h, cs, first=first)
            fused(n, slot, diag, h, cs)

    def lagged_sweep():
        d0, d1 = 2 * i, 2 * i + 1
        for h, cs in lanes:
            scores(d0, 0, 0, h, cs)
            if cs.stop > tk:
                scores(d1, 1, 1, h, cs)
        for h, cs in lanes:
            off0, off1 = off_fn(d0), off_fn(d1)
            ref = mt_ref[0, h][:, cs] if off0 is None else mt_ref[0, h][:, cs] + off0
            if cs.stop > tk:
                mt1 = mt_ref[1, h][:, cs]
                ref = jnp.maximum(ref, mt1 if off1 is None else mt1 + off1)
                p1 = jnp.exp(s_ref[1, h][:, cs] - (ref if off1 is None else ref - off1)).astype(BF16)
            else:
                p1 = jnp.zeros((tk, CHUNK), BF16)
            p_ref[1, h][:, cs] = jnp.exp(s_ref[0, h][:, cs] - (ref if off0 is None else ref - off0)).astype(BF16)
            p_ref[0, h][:, cs] = p1
            m_ref[h, :, cs] = ref
            a_ref[0, h][:, cs] = jnp.ones((1, CHUNK), F32)
            g_ref[h, :, cs] = jnp.ones((1, CHUNK), F32)
            bad_ref[h, :, cs] = jnp.full((1, CHUNK), NEG_INF, F32)
            values(d0, 1, h, cs, first=True)

        def two_steps(u, carry):
            lagged_step(2 * u + 2, 1)
            lagged_step(2 * u + 3, 0)
            return carry

        lax.fori_loop(0, i, two_steps, 0)
        for h, cs in lanes:
            values(tile_of(2 * i + 1), 0, h, cs)

    lagged_sweep()
    worst = jnp.max(bad_ref[...])

    @pl.when(jnp.logical_not(worst <= LAG_LIMIT))
    def _():
        exact_sweep()

    return acc_ref


def _fox_kernel(q_ref, k_ref, v_ref, g_ref, cb_ref, o_ref, *scratch, tq, tk):
    i = pl.program_id(1)

    @pl.when(i == 0)
    def _():
        _fill_vt(v_ref, scratch[0], tk)

    acc_ref = _pair_flash(q_ref, k_ref, lambda j0: cb_ref[pl.ds(j0, tk), :], lambda j: None, i, scratch,
                          tq=tq, tk=tk, v_rows=((0, HEAD_DIM), (HEAD_DIM, LANES)),
                          bias_rows=((0, N_SPLIT), (N_SPLIT, 2 * N_SPLIT)), bias_sign=-1.0)
    d = HEAD_DIM
    y = jnp.concatenate([acc_ref[0, :d] / acc_ref[0, d:d + 1], acc_ref[1, :d] / acc_ref[1, d:d + 1]], axis=0).T
    o_ref[...] = (y * _silu(g_ref[...].astype(F32))).astype(o_ref.dtype)


def _fox_attention(z, cb, *, tq, tk):
    s = z.shape[0]
    nb = C_WIDTH // LANES
    return pl.pallas_call(
        functools.partial(_fox_kernel, tq=tq, tk=tk),
        grid=(nb, s // tq),
        in_specs=[
            pl.BlockSpec((tq, LANES), lambda p, i: (i, p)),
            pl.BlockSpec((s, LANES), lambda p, i: (0, nb + p)),
            pl.BlockSpec((s, LANES), lambda p, i: (0, 2 * nb + p)),
            pl.BlockSpec((tq, LANES), lambda p, i: (i, 3 * nb + p)),
            pl.BlockSpec((s, LANES), lambda p, i: (0, p)),
        ],
        out_specs=pl.BlockSpec((tq, LANES), lambda p, i: (i, p)),
        out_shape=jax.ShapeDtypeStruct((s, C_WIDTH), BF16),
        scratch_shapes=_flash_scratch(s, tq, tk, HEAD_DIM),
        compiler_params=_cparams(("parallel", "arbitrary")),
    )(z, z, z, z, cb)


def _diff_kernel(q_ref, k_ref, v_ref, g_ref, slope_ref, lam_ref, sg_ref, o_ref, kb_ref, *scratch,
                 tq, tk, lam_init):
    i = pl.program_id(1)
    slope = slope_ref[...]

    @pl.when(i == 0)
    def _():
        _fill_vt(v_ref, scratch[0], tk)
        r = lax.broadcasted_iota(jnp.int32, (tk, LANES), 0).astype(F32)
        lane = lax.broadcasted_iota(jnp.int32, (tk, LANES), 1)
        terms = _split_bf16(slope * r)
        kb = jnp.zeros((tk, LANES), F32)
        for n, t in enumerate(terms):
            kb = jnp.where(lane == n, t.astype(F32), kb)
        kb_ref[...] = kb.astype(BF16)

    def off_fn(j):
        return slope * (j * tk - i * tq).astype(F32)

    acc_ref = _pair_flash(q_ref, k_ref, lambda j0: kb_ref[...], off_fn, i, scratch,
                          tq=tq, tk=tk, v_rows=((0, LANES + ONES_ROWS), (0, LANES + ONES_ROWS)),
                          bias_rows=((0, N_SPLIT), (0, N_SPLIT)), bias_sign=1.0)
    lp = lam_ref[...]
    lam = (jnp.exp(jnp.sum(lp[0:1] * lp[1:2], axis=-1, keepdims=True))
           - jnp.exp(jnp.sum(lp[2:3] * lp[3:4], axis=-1, keepdims=True)) + lam_init)
    d = LANES
    o = (acc_ref[0, :d] / acc_ref[0, d:d + 1] - lam * (acc_ref[1, :d] / acc_ref[1, d:d + 1])).T
    o = o * lax.rsqrt(jnp.mean(o * o, axis=-1, keepdims=True) + RMS_EPS) * sg_ref[...]
    o = o * (1.0 - lam_init)
    o_ref[...] = (o * _silu(g_ref[...].astype(F32))).astype(o_ref.dtype)


def _diff_attention(z, slopes, lam_params, subln_g, lam_init, *, tq, tk):
    s = z.shape[0]
    qb, kb, vb, gb = 18, 26, 34, 42
    return pl.pallas_call(
        functools.partial(_diff_kernel, tq=tq, tk=tk, lam_init=lam_init),
        grid=(B_HEADS, s // tq),
        in_specs=[
            pl.BlockSpec((tq, LANES), lambda h, i: (i, qb + h)),
            pl.BlockSpec((s, LANES), lambda h, i: (0, kb + h)),
            pl.BlockSpec((s, LANES), lambda h, i: (0, vb + h)),
            pl.BlockSpec((tq, LANES), lambda h, i: (i, gb + h)),
            pl.BlockSpec((None, 1, 1), lambda h, i: (h, 0, 0)),
            pl.BlockSpec((4, HEAD_DIM), lambda h, i: (0, 0)),
            pl.BlockSpec((1, LANES), lambda h, i: (0, 0)),
        ],
        out_specs=pl.BlockSpec((tq, LANES), lambda h, i: (i, h)),
        out_shape=jax.ShapeDtypeStruct((s, B_WIDTH), BF16),
        scratch_shapes=[pltpu.VMEM((tk, LANES), BF16)] + _flash_scratch(s, tq, tk, LANES),
        compiler_params=_cparams(("parallel", "arbitrary")),
    )(z, z, z, z, slopes.reshape(B_HEADS, 1, 1), lam_params, subln_g.reshape(1, LANES))


def _swa_kernel(slope_ref, sink_ref, q_ref, kp_ref, kc_ref, vp_ref, vc_ref, g_ref, o_ref):
    i = pl.program_id(0)
    blk = WINDOW
    lane = lax.broadcasted_iota(jnp.int32, (1, LANES), 1)
    lo = lane < HEAD_DIM

    def swap_halves(a):
        return pltpu.roll(a.astype(F32), HEAD_DIM, 1).astype(BF16)

    k = jnp.concatenate([kp_ref[...], kc_ref[...]], axis=0)
    v = jnp.concatenate([vp_ref[...], vc_ref[...]], axis=0)
    k_sw, v_sw = swap_halves(k), swap_halves(v)

    r = lax.broadcasted_iota(jnp.int32, (blk, 2 * blk), 0)
    c = lax.broadcasted_iota(jnp.int32, (blk, 2 * blk), 1)
    dist = r - c + blk
    valid = (dist >= 0) & (dist < WINDOW) & ((c >= blk) | (i > 0))
    dist_f = dist.astype(F32)

    for pair in range(A_HEADS // 2):
        qp = q_ref[:, pair * LANES:(pair + 1) * LANES] * jnp.asarray(SCALE, BF16)
        zero = jnp.zeros_like(qp)
        outs = []
        for a in range(2):
            h = 2 * pair + a
            kv = h // A_GROUP
            qh = jnp.where(lo, qp, zero) if a == 0 else jnp.where(lo, zero, qp)
            kk, vv = (k, v) if a == kv else (k_sw, v_sw)
            s = lax.dot_general(qh, kk, (((1,), (1,)), ((), ())), preferred_element_type=F32)
            s = jnp.where(valid, s - slope_ref[h] * dist_f, NEG_INF)
            sink = sink_ref[h]
            m = jnp.maximum(jnp.max(s, axis=-1, keepdims=True), sink)
            e = jnp.exp(s - m)
            denom = jnp.sum(e, axis=-1, keepdims=True) + jnp.exp(sink - m)
            p = (e / denom).astype(BF16)
            outs.append(jnp.dot(p, vv, preferred_element_type=F32))
        y = jnp.where(lo, outs[0], outs[1])
        g = g_ref[:, pair * LANES:(pair + 1) * LANES].astype(F32)
        o_ref[:, pair * LANES:(pair + 1) * LANES] = (y * _silu(g)).astype(o_ref.dtype)


def _swa_attention(z, slopes, sinks):
    s = z.shape[0]
    blk = WINDOW
    nq = A_WIDTH // LANES
    kb, vb = nq, nq + 1
    g_col = (nq + 2) * LANES
    smem = pl.BlockSpec(memory_space=pltpu.SMEM)
    prev = lambda i: (jnp.maximum(i - 1, 0), kb)
    prev_v = lambda i: (jnp.maximum(i - 1, 0), vb)
    return pl.pallas_call(
        _swa_kernel,
        grid=(s // blk,),
        in_specs=[
            smem, smem,
            pl.BlockSpec((blk, A_WIDTH), lambda i: (i, 0)),
            pl.BlockSpec((blk, LANES), prev),
            pl.BlockSpec((blk, LANES), lambda i: (i, kb)),
            pl.BlockSpec((blk, LANES), prev_v),
            pl.BlockSpec((blk, LANES), lambda i: (i, vb)),
            pl.BlockSpec((blk, A_WIDTH), lambda i: (i, 0)),
        ],
        out_specs=pl.BlockSpec((blk, A_WIDTH), lambda i: (i, 0)),
        out_shape=jax.ShapeDtypeStruct((s, A_WIDTH), BF16),
        compiler_params=_cparams(("parallel",)),
    )(slopes, sinks, z, z, z, z, z, z[:, g_col:g_col + A_WIDTH])


def _out_kernel(*refs, n_y, final):
    x_ref = refs[0]
    y_refs = refs[1:1 + n_y]
    w_refs = refs[1 + n_y:1 + 2 * n_y]
    p_ref, wp_ref, gn_ref, wg_ref = refs[1 + 2 * n_y:5 + 2 * n_y]
    if final:
        fg_ref, o_ref = refs[5 + 2 * n_y:]
    else:
        (o_ref,) = refs[5 + 2 * n_y:]

    x1 = x_ref[...]
    for y_ref, w_ref in zip(y_refs, w_refs):
        x1 = x1 + jnp.dot(y_ref[...], w_ref[...], preferred_element_type=F32)
    hn = x1 * lax.rsqrt(jnp.mean(x1 * x1, axis=-1, keepdims=True) + RMS_EPS) * gn_ref[...]
    gate = jax.nn.sigmoid(jnp.dot(hn.astype(BF16), wg_ref[...], preferred_element_type=F32))
    pp = jnp.dot(p_ref[...].astype(BF16), wp_ref[...], preferred_element_type=F32)
    x2 = x1 + gate * pp
    if final:
        x2 = x2 * lax.rsqrt(jnp.mean(x2 * x2, axis=-1, keepdims=True) + RMS_EPS) * fg_ref[...]
    o_ref[...] = x2


def _out_block(x, ys, ws, p, wp, gn, wg, final_g=None, *, tm):
    s, d = x.shape
    n_y = len(ys)
    final = final_g is not None
    const = lambda i: (0, 0)
    single = pl.Buffered(1)
    in_specs = [pl.BlockSpec((tm, d), lambda i: (i, 0))]
    in_specs += [pl.BlockSpec((tm, y.shape[1]), lambda i: (i, 0)) for y in ys]
    in_specs += [pl.BlockSpec(w.shape, const, pipeline_mode=single) for w in ws]
    in_specs += [
        pl.BlockSpec((tm, p.shape[1]), lambda i: (i, 0)),
        pl.BlockSpec(wp.shape, const, pipeline_mode=single),
        pl.BlockSpec((1, d), const),
        pl.BlockSpec(wg.shape, const, pipeline_mode=single),
    ]
    args = [x, *ys, *ws, p, wp, gn.reshape(1, d), wg]
    if final:
        in_specs.append(pl.BlockSpec((1, d), const))
        args.append(final_g.reshape(1, d))
    return pl.pallas_call(
        functools.partial(_out_kernel, n_y=n_y, final=final),
        grid=(s // tm,),
        in_specs=in_specs,
        out_specs=pl.BlockSpec((tm, d), lambda i: (i, 0)),
        out_shape=jax.ShapeDtypeStruct((s, d), F32),
        compiler_params=_cparams(("parallel",)),
    )(*args)


def _alibi_slopes(n):
    return jnp.asarray([2.0 ** (-8.0 * (h + 1) / n) for h in range(n)], dtype=F32)


def kernel(x, p, norm_g, w_in_ab, w_out_ab, attn_sinks, diff_lambda, diff_subln_g,
           w_in_c, w_out_c, forget_bias, ple_proj, ple_gate, ple_norm_g, final_norm_g):
    b, s, d = x.shape
    assert (b, s, d) == (1, SEQ, D_MODEL)
    xs = x.reshape(s, d)
    for i in range(DEPTH):
        j = i // 2
        last = i == DEPTH - 1
        if i % 2 == 0:
            lam_init = 0.8 - 0.6 * math.exp(-0.3 * i)
            z = _norm_matmul(xs, norm_g[i], w_in_ab[j].astype(BF16), tm=512, tn=1280)
            ya = _swa_attention(z, _alibi_slopes(A_HEADS), attn_sinks[j].astype(F32))
            yb = _diff_attention(z, _alibi_slopes(B_HEADS), diff_lambda[j].astype(F32),
                                 diff_subln_g[j].astype(F32), lam_init, tq=1024, tk=512)
            w_out = w_out_ab[j].astype(BF16)
            ys, ws = [ya, yb], [w_out[:A_WIDTH], w_out[A_WIDTH:]]
        else:
            w = w_in_c[j]
            w_main = jnp.concatenate([w[:, :3 * C_WIDTH], w[:, 3 * C_WIDTH + C_HEADS:]], axis=1).astype(BF16)
            w_f = jnp.pad(w[:, 3 * C_WIDTH:3 * C_WIDTH + C_HEADS], ((0, 0), (0, LANES - C_HEADS))).astype(BF16)
            z, f = _norm_matmul(xs, norm_g[i], w_main, w_f, tm=512, tn=1024)
            f_bias = jnp.pad(forget_bias[j].astype(F32), (0, LANES - C_HEADS)).reshape(1, LANES)
            cb = _logsig_cumsum(f, f_bias, C_HEADS // 2)
            ys, ws = [_fox_attention(z, cb, tq=1024, tk=512)], [w_out_c[j].astype(BF16)]
        xs = _out_block(xs, ys, ws, p[i].reshape(s, D_PLE), ple_proj[i].astype(BF16),
                        ple_norm_g[i], ple_gate[i].astype(BF16),
                        final_norm_g if last else None, tm=256)
    return xs.reshape(b, s, d)
```

```python
import functools
import itertools
import math

import jax
import jax.numpy as jnp
from jax import lax
from jax.experimental import pallas as pl
from jax.experimental.pallas import tpu as pltpu

F32 = jnp.float32
BF16 = jnp.bfloat16

D_MODEL = 2048
SEQ = 8192
DEPTH = 2
HEAD_DIM = 64
D_PLE = 256
WINDOW = 128
RMS_EPS = 1e-6
A_HEADS = 16
A_KV_HEADS = 2
A_GROUP = A_HEADS // A_KV_HEADS
A_WIDTH = A_HEADS * HEAD_DIM
B_HEADS = 8
B_WIDTH = B_HEADS * 2 * HEAD_DIM
C_HEADS = 32
C_WIDTH = C_HEADS * HEAD_DIM
AB_IN = 6400
SCALE = HEAD_DIM ** -0.5

LANES = 128
NEG_INF = float("-inf")
VMEM_LIMIT = 48 * 1024 * 1024


def _cparams(sem):
    return pltpu.CompilerParams(dimension_semantics=sem, vmem_limit_bytes=VMEM_LIMIT)


def _silu(g):
    return g * (1.0 / (1.0 + jnp.exp(-g)))


def _norm_matmul_kernel(x_ref, g_ref, *rest, seg_ends, with_extra):
    w_refs, rest = rest[:len(seg_ends)], rest[len(seg_ends):]
    if with_extra:
        we_ref, o_ref, oe_ref, h_ref = rest
    else:
        o_ref, h_ref = rest
    j = pl.program_id(1)

    @pl.when(j == 0)
    def _():
        x = x_ref[...]
        ms = jnp.mean(x * x, axis=-1, keepdims=True)
        h = (x * lax.rsqrt(ms + RMS_EPS) * g_ref[...]).astype(BF16)
        h_ref[...] = h
        if with_extra:
            oe_ref[...] = jnp.dot(h, we_ref[...], preferred_element_type=F32)

    def project(w_ref):
        o_ref[...] = jnp.dot(h_ref[...], w_ref[...], preferred_element_type=F32).astype(o_ref.dtype)

    if len(w_refs) == 1:
        project(w_refs[0])
    else:
        for lo, hi, w_ref in zip((0,) + seg_ends[:-1], seg_ends, w_refs):
            pl.when((j >= lo) & (j < hi))(functools.partial(project, w_ref))


def _norm_matmul(x, g, segments, extra=None, *, tm, tn):
    s, d = x.shape
    seg_ends = tuple(itertools.accumulate(count for _, _, count in segments))
    with_extra = extra is not None
    in_specs = [
        pl.BlockSpec((tm, d), lambda i, j: (i, 0)),
        pl.BlockSpec((1, d), lambda i, j: (0, 0)),
    ]
    args = [x, g.reshape(1, d)]
    for (w, first, count), end in zip(segments, seg_ends):
        in_specs.append(pl.BlockSpec(
            (d, tn), lambda i, j, first=first, count=count, lo=end - count:
            (0, first + jnp.clip(j - lo, 0, count - 1))))
        args.append(w)
    n = seg_ends[-1] * tn
    out_shape = [jax.ShapeDtypeStruct((s, n), BF16)]
    out_specs = [pl.BlockSpec((tm, tn), lambda i, j: (i, j))]
    if with_extra:
        we, tile = extra
        in_specs.append(pl.BlockSpec((d, LANES), lambda i, j: (0, tile)))
        out_shape.append(jax.ShapeDtypeStruct((s, LANES), F32))
        out_specs.append(pl.BlockSpec((tm, LANES), lambda i, j: (i, 0)))
        args.append(we)
    res = pl.pallas_call(
        functools.partial(_norm_matmul_kernel, seg_ends=seg_ends, with_extra=with_extra),
        grid=(s // tm, n // tn),
        in_specs=in_specs,
        out_specs=out_specs,
        out_shape=out_shape,
        scratch_shapes=[pltpu.VMEM((tm, d), BF16)],
        compiler_params=_cparams(("parallel", "arbitrary")),
    )(*args)
    return res if with_extra else res[0]


N_SPLIT = 3


def _split_bf16(x):
    parts = []
    for _ in range(N_SPLIT):
        t = x.astype(BF16)
        parts.append(t)
        x = x - t.astype(F32)
    return parts


def _logsig_cumsum_kernel(f_ref, b_ref, o_ref, carry_ref, sel_ref, *, blk):
    n = f_ref.shape[1]

    @pl.when(pl.program_id(0) == 0)
    def _():
        carry_ref[...] = jnp.zeros_like(carry_ref)
        r = lax.broadcasted_iota(jnp.int32, sel_ref.shape, 0)
        l = lax.broadcasted_iota(jnp.int32, sel_ref.shape, 1)
        part, head = r // n, r % n
        lane = jnp.where(l // LANES == head // 2, l % LANES, -1)
        sel_ref[...] = jnp.where(lane == N_SPLIT * (head % 2) + part, 1.0, 0.0).astype(BF16)

    z = f_ref[...] + b_ref[...]
    ls = jnp.minimum(z, 0.0) - jnp.log1p(jnp.exp(-jnp.abs(z)))
    row = lax.broadcasted_iota(jnp.int32, (blk, blk), 0)
    col = lax.broadcasted_iota(jnp.int32, (blk, blk), 1)
    lower = (col <= row).astype(F32)
    c = jnp.dot(lower, ls, preferred_element_type=F32,
                precision=lax.Precision.HIGHEST) + carry_ref[...]
    carry_ref[...] = c[blk - 1:blk, :]
    terms = jnp.concatenate(_split_bf16(c), axis=1)
    o_ref[...] = jnp.dot(terms, sel_ref[...], preferred_element_type=F32).astype(BF16)


def _logsig_cumsum(f, bias_row, n_pairs, *, blk=256):
    s, n = f.shape
    return pl.pallas_call(
        functools.partial(_logsig_cumsum_kernel, blk=blk),
        grid=(s // blk,),
        in_specs=[pl.BlockSpec((blk, n), lambda i: (i, 0)),
                  pl.BlockSpec((1, n), lambda i: (0, 0))],
        out_specs=pl.BlockSpec((blk, n_pairs * LANES), lambda i: (i, 0)),
        out_shape=jax.ShapeDtypeStruct((s, n_pairs * LANES), BF16),
        scratch_shapes=[pltpu.VMEM((1, n), F32), pltpu.VMEM((N_SPLIT * n, n_pairs * LANES), BF16)],
        compiler_params=_cparams(("arbitrary",)),
    )(f, bias_row)


ONES_ROWS = 16
CHUNK = 256
LAG_LIMIT = 30.0


def _flash_scratch(s, tq, tk, rows):
    return [
        pltpu.VMEM((s // tk, LANES + ONES_ROWS, tk), BF16),
        pltpu.VMEM((2, 2 * LANES, tq), BF16),
        pltpu.VMEM((2, 1, tq), F32),
        pltpu.VMEM((2, rows + ONES_ROWS, tq), F32),
        pltpu.VMEM((2, 1, tq), F32),
        pltpu.VMEM((2, 1, tq), F32),
        *[pltpu.VMEM((tk, tq), F32) for _ in range(4)],
        *[pltpu.VMEM((1, tq), F32) for _ in range(4)],
        *[pltpu.VMEM((tk, tq), BF16) for _ in range(4)],
        *[pltpu.VMEM((1, tq), F32) for _ in range(4)],
    ]


def _fill_vt(v_ref, vt_ref, tk):
    def body(c, carry):
        vt_ref[c, :LANES, :] = v_ref[pl.ds(pl.multiple_of(c * tk, tk), tk), :].T
        vt_ref[c, LANES:, :] = jnp.ones((ONES_ROWS, tk), BF16)
        return carry
    lax.fori_loop(0, vt_ref.shape[0], body, 0)


def _pair_flash(q_ref, k_ref, kb_fn, off_fn, i, scratch, *, tq, tk, v_rows, bias_rows, bias_sign):
    assert tq == 2 * tk
    vt_ref, qt_ref, m_ref, acc_ref, g_ref, bad_ref = scratch[:6]
    s_ref, mt_ref, p_ref, a_ref = ({(slot, h): scratch[6 + 4 * n + 2 * slot + h]
                                    for slot in range(2) for h in range(2)} for n in range(4))
    row = lax.broadcasted_iota(jnp.int32, (LANES, 1), 0)
    lo = row < HEAD_DIM
    qt = (q_ref[...] * jnp.asarray(SCALE, BF16)).T
    zero = jnp.zeros_like(qt)
    for h in range(2):
        qt_ref[h, :LANES, :] = jnp.where(lo, qt, zero) if h == 0 else jnp.where(lo, zero, qt)
        b0, b1 = bias_rows[h]
        qt_ref[h, LANES:, :] = jnp.where((row >= b0) & (row < b1), bias_sign, 0.0).astype(BF16) + zero

    lanes = [(h, slice(c, c + CHUNK)) for c in range(0, tq, CHUNK) for h in range(2)]

    def tile_of(n):
        return jnp.where(n < 2, 2 * i + n, n - 2)

    def raw_scores(j, diag, h, cs):
        j0 = pl.multiple_of(j * tk, tk)
        lhs = jnp.concatenate([k_ref[pl.ds(j0, tk), :], kb_fn(j0)], axis=1)
        t = jnp.dot(lhs, qt_ref[h, :, cs], preferred_element_type=F32)
        if diag is not None:
            kpos = lax.broadcasted_iota(jnp.int32, t.shape, 0) + diag * tk
            qpos = lax.broadcasted_iota(jnp.int32, t.shape, 1) + cs.start
            t = jnp.where(kpos <= qpos, t, NEG_INF)
        return t

    def scores(j, slot, diag, h, cs):
        t = raw_scores(j, diag, h, cs)
        s_ref[slot, h][:, cs] = t
        mt_ref[slot, h][:, cs] = jnp.max(t, axis=0, keepdims=True)

    def values(j, slot, h, cs, first=False):
        r0, r1 = v_rows[h]
        vt = vt_ref[j, r0:r1, :]
        if r1 != LANES + ONES_ROWS:
            vt = jnp.concatenate([vt, vt_ref[j, LANES:, :]], axis=0)
        pv = jnp.dot(vt, p_ref[slot, h][:, cs], preferred_element_type=F32)
        acc_ref[h, :, cs] = pv if first else acc_ref[h, :, cs] * a_ref[slot, h][:, cs] + pv

    def softmax(j, slot, h, cs):
        off = off_fn(j)
        mt = mt_ref[slot, h][:, cs]
        m_old = m_ref[h, :, cs]
        m_new = jnp.maximum(m_old, mt if off is None else mt + off)
        a_ref[1 - slot, h][:, cs] = jnp.exp(m_old - m_new)
        m_ref[h, :, cs] = m_new
        shift = m_new if off is None else m_new - off
        p_ref[1 - slot, h][:, cs] = jnp.exp(s_ref[slot, h][:, cs] - shift).astype(BF16)

    def exact_step(t, slot, diag=None, first=False, last=False):
        for h, cs in lanes:
            if not first:
                values(tile_of(t - 1), slot, h, cs)
            softmax(tile_of(t), slot, h, cs)
            if not last:
                scores(tile_of(t + 1), 1 - slot, diag, h, cs)

    def exact_sweep():
        for h in range(2):
            m_ref[h] = jnp.full((1, tq), NEG_INF, F32)
            acc_ref[h] = jnp.zeros(acc_ref.shape[1:], F32)
        for h, cs in lanes:
            scores(2 * i, 0, 0, h, cs)
        exact_step(0, 0, 1, first=True)

        def two_steps(u, carry):
            exact_step(2 * u + 1, 1)
            exact_step(2 * u + 2, 0)
            return carry

        lax.fori_loop(0, i, two_steps, 0)
        exact_step(2 * i + 1, 1, last=True)
        for h, cs in lanes:
            values(tile_of(2 * i + 1), 0, h, cs)

    def fused(j, slot, diag, h, cs):
        off = off_fn(j)
        t = raw_scores(j, diag, h, cs)
        ref = m_ref[h, :, cs]
        p_ref[slot, h][:, cs] = jnp.exp(t - (ref if off is None else ref - off)).astype(BF16)
        mt = jnp.max(t, axis=0, keepdims=True)
        mt = mt if off is None else mt + off
        a_ref[slot, h][:, cs] = g_ref[h, :, cs]
        m_new = jnp.maximum(ref, mt)
        g_ref[h, :, cs] = jnp.exp(ref - m_new)
        m_ref[h, :, cs] = m_new
        bad_ref[h, :, cs] = jnp.maximum(bad_ref[h, :, cs], mt - ref)

    def lag_tile(n):
        return jnp.where(n < 2, 2 * i, n - 2)

    def lagged_step(n, slot):
        for h, cs in lanes:
            values(lag_tile(n - 1), 1 - slot, h, cs)
            fused(lag_tile(n), slot, None, h, cs)

    def lagged_sweep():
        d0, d1 = 2 * i, 2 * i + 1
        high = lambda cs: cs.stop > tk
        for h, cs in lanes:
            scores(d1 if high(cs) else d0, 0, 1 if high(cs) else 0, h, cs)
        for h, cs in lanes:
            off = off_fn(d1 if high(cs) else d0)
            mt = mt_ref[0, h][:, cs]
            p_ref[1, h][:, cs] = jnp.exp(s_ref[0, h][:, cs] - mt).astype(BF16)
            m_ref[h, :, cs] = mt if off is None else mt + off
            g_ref[h, :, cs] = jnp.ones((1, CHUNK), F32)
            bad_ref[h, :, cs] = jnp.full((1, CHUNK), NEG_INF, F32)
        for h, cs in lanes:
            values(d1 if high(cs) else d0, 1, h, cs, first=True)
            if high(cs):
                fused(d0, 0, None, h, cs)
            else:
                p_ref[0, h][:, cs] = jnp.zeros((tk, CHUNK), BF16)
                a_ref[0, h][:, cs] = jnp.ones((1, CHUNK), F32)

        def two_steps(u, carry):
            lagged_step(2 * u + 2, 1)
            lagged_step(2 * u + 3, 0)
            return carry

        lax.fori_loop(0, i, two_steps, 0)
        for h, cs in lanes:
            values(lag_tile(2 * i + 1), 0, h, cs)

    lagged_sweep()
    worst = jnp.max(bad_ref[...])

    @pl.when(jnp.logical_not(worst <= LAG_LIMIT))
    def _():
        exact_sweep()

    return acc_ref


def _fox_kernel(q_ref, k_ref, v_ref, g_ref, cb_ref, o_ref, *scratch, tq, tk):
    i = pl.program_id(1)

    @pl.when(i == 0)
    def _():
        _fill_vt(v_ref, scratch[0], tk)

    acc_ref = _pair_flash(q_ref, k_ref, lambda j0: cb_ref[pl.ds(j0, tk), :], lambda j: None, i, scratch,
                          tq=tq, tk=tk, v_rows=((0, HEAD_DIM), (HEAD_DIM, LANES)),
                          bias_rows=((0, N_SPLIT), (N_SPLIT, 2 * N_SPLIT)), bias_sign=-1.0)
    d = HEAD_DIM
    y = jnp.concatenate([acc_ref[0, :d] / acc_ref[0, d:d + 1], acc_ref[1, :d] / acc_ref[1, d:d + 1]], axis=0).T
    o_ref[...] = (y * _silu(g_ref[...].astype(F32))).astype(o_ref.dtype)


def _fox_attention(z, cb, *, tq, tk):
    s = z.shape[0]
    nb = C_WIDTH // LANES
    return pl.pallas_call(
        functools.partial(_fox_kernel, tq=tq, tk=tk),
        grid=(nb, s // tq),
        in_specs=[
            pl.BlockSpec((tq, LANES), lambda p, i: (i, p)),
            pl.BlockSpec((s, LANES), lambda p, i: (0, nb + p)),
            pl.BlockSpec((s, LANES), lambda p, i: (0, 2 * nb + p)),
            pl.BlockSpec((tq, LANES), lambda p, i: (i, 3 * nb + p)),
            pl.BlockSpec((s, LANES), lambda p, i: (0, p)),
        ],
        out_specs=pl.BlockSpec((tq, LANES), lambda p, i: (i, p)),
        out_shape=jax.ShapeDtypeStruct((s, C_WIDTH), BF16),
        scratch_shapes=_flash_scratch(s, tq, tk, HEAD_DIM),
        compiler_params=_cparams(("parallel", "arbitrary")),
    )(z, z, z, z, cb)


def _diff_kernel(q_ref, k_ref, v_ref, g_ref, slope_ref, lam_ref, sg_ref, o_ref, kb_ref, *scratch,
                 tq, tk, lam_init):
    i = pl.program_id(1)
    slope = slope_ref[...]

    @pl.when(i == 0)
    def _():
        _fill_vt(v_ref, scratch[0], tk)
        r = lax.broadcasted_iota(jnp.int32, (tk, LANES), 0).astype(F32)
        lane = lax.broadcasted_iota(jnp.int32, (tk, LANES), 1)
        terms = _split_bf16(slope * r)
        kb = jnp.zeros((tk, LANES), F32)
        for n, t in enumerate(terms):
            kb = jnp.where(lane == n, t.astype(F32), kb)
        kb_ref[...] = kb.astype(BF16)

    def off_fn(j):
        return slope * (j * tk - i * tq).astype(F32)

    acc_ref = _pair_flash(q_ref, k_ref, lambda j0: kb_ref[...], off_fn, i, scratch,
                          tq=tq, tk=tk, v_rows=((0, LANES + ONES_ROWS), (0, LANES + ONES_ROWS)),
                          bias_rows=((0, N_SPLIT), (0, N_SPLIT)), bias_sign=1.0)
    lp = lam_ref[...]
    lam = (jnp.exp(jnp.sum(lp[0:1] * lp[1:2], axis=-1, keepdims=True))
           - jnp.exp(jnp.sum(lp[2:3] * lp[3:4], axis=-1, keepdims=True)) + lam_init)
    d = LANES
    o = (acc_ref[0, :d] / acc_ref[0, d:d + 1] - lam * (acc_ref[1, :d] / acc_ref[1, d:d + 1])).T
    o = o * lax.rsqrt(jnp.mean(o * o, axis=-1, keepdims=True) + RMS_EPS) * sg_ref[...]
    o = o * (1.0 - lam_init)
    o_ref[...] = (o * _silu(g_ref[...].astype(F32))).astype(o_ref.dtype)


def _diff_attention(z, slopes, lam_params, subln_g, lam_init, *, tq, tk):
    s = z.shape[0]
    qb, kb, vb, gb = 18, 26, 34, 42
    return pl.pallas_call(
        functools.partial(_diff_kernel, tq=tq, tk=tk, lam_init=lam_init),
        grid=(B_HEADS, s // tq),
        in_specs=[
            pl.BlockSpec((tq, LANES), lambda h, i: (i, qb + h)),
            pl.BlockSpec((s, LANES), lambda h, i: (0, kb + h)),
            pl.BlockSpec((s, LANES), lambda h, i: (0, vb + h)),
            pl.BlockSpec((tq, LANES), lambda h, i: (i, gb + h)),
            pl.BlockSpec((None, 1, 1), lambda h, i: (h, 0, 0)),
            pl.BlockSpec((4, HEAD_DIM), lambda h, i: (0, 0)),
            pl.BlockSpec((1, LANES), lambda h, i: (0, 0)),
        ],
        out_specs=pl.BlockSpec((tq, LANES), lambda h, i: (i, h)),
        out_shape=jax.ShapeDtypeStruct((s, B_WIDTH), BF16),
        scratch_shapes=[pltpu.VMEM((tk, LANES), BF16)] + _flash_scratch(s, tq, tk, LANES),
        compiler_params=_cparams(("parallel", "arbitrary")),
    )(z, z, z, z, slopes.reshape(B_HEADS, 1, 1), lam_params, subln_g.reshape(1, LANES))


def _swa_kernel(slope_ref, sink_ref, q_ref, kp_ref, kc_ref, vp_ref, vc_ref, *rest):
    *g_refs, o_ref = rest
    i = pl.program_id(0)
    blk = WINDOW
    lane = lax.broadcasted_iota(jnp.int32, (1, LANES), 1)
    lo = lane < HEAD_DIM

    def swap_halves(a):
        return pltpu.roll(a.astype(F32), HEAD_DIM, 1).astype(BF16)

    k = jnp.concatenate([kp_ref[...], kc_ref[...]], axis=0)
    v = jnp.concatenate([vp_ref[...], vc_ref[...]], axis=0)
    k_sw, v_sw = swap_halves(k), swap_halves(v)

    r = lax.broadcasted_iota(jnp.int32, (blk, 2 * blk), 0)
    c = lax.broadcasted_iota(jnp.int32, (blk, 2 * blk), 1)
    dist = r - c + blk
    valid = (dist >= 0) & (dist < WINDOW) & ((c >= blk) | (i > 0))
    neg_dist = jnp.where(valid, -dist.astype(F32), NEG_INF)

    for pair in range(A_HEADS // 2):
        qp = q_ref[:, pair * LANES:(pair + 1) * LANES] * jnp.asarray(SCALE, BF16)
        zero = jnp.zeros_like(qp)
        outs = []
        for a in range(2):
            h = 2 * pair + a
            kv = h // A_GROUP
            qh = jnp.where(lo, qp, zero) if a == 0 else jnp.where(lo, zero, qp)
            kk, vv = (k, v) if a == kv else (k_sw, v_sw)
            s = lax.dot_general(qh, kk, (((1,), (1,)), ((), ())), preferred_element_type=F32)
            s = s + slope_ref[h] * neg_dist
            sink = sink_ref[h]
            m = jnp.maximum(jnp.max(s, axis=-1, keepdims=True), sink)
            e = jnp.exp(s - m)
            denom = jnp.sum(e, axis=-1, keepdims=True) + jnp.exp(sink - m)
            outs.append(jnp.dot(e.astype(BF16), vv, preferred_element_type=F32) / denom)
        y = jnp.where(lo, outs[0], outs[1])
        g_ref = g_refs[pair // 2]
        g = g_ref[:, (pair % 2) * LANES:(pair % 2 + 1) * LANES].astype(F32)
        o_ref[:, pair * LANES:(pair + 1) * LANES] = (y * _silu(g)).astype(o_ref.dtype)


def _swa_attention(z, slopes, sinks):
    s = z.shape[0]
    blk = WINDOW
    nq = A_WIDTH // LANES
    kb, vb = nq, nq + 1
    gw = 2 * LANES
    g_first = (nq + 2) * LANES // gw
    assert g_first * gw == (nq + 2) * LANES
    smem = pl.BlockSpec(memory_space=pltpu.SMEM)
    prev = lambda i: (jnp.maximum(i - 1, 0), kb)
    prev_v = lambda i: (jnp.maximum(i - 1, 0), vb)
    return pl.pallas_call(
        _swa_kernel,
        grid=(s // blk,),
        in_specs=[
            smem, smem,
            pl.BlockSpec((blk, A_WIDTH), lambda i: (i, 0)),
            pl.BlockSpec((blk, LANES), prev),
            pl.BlockSpec((blk, LANES), lambda i: (i, kb)),
            pl.BlockSpec((blk, LANES), prev_v),
            pl.BlockSpec((blk, LANES), lambda i: (i, vb)),
            *[pl.BlockSpec((blk, gw), lambda i, c=g_first + n: (i, c)) for n in range(A_WIDTH // gw)],
        ],
        out_specs=pl.BlockSpec((blk, A_WIDTH), lambda i: (i, 0)),
        out_shape=jax.ShapeDtypeStruct((s, A_WIDTH), BF16),
        compiler_params=_cparams(("parallel",)),
    )(slopes, sinks, *([z] * (5 + A_WIDTH // gw)))


def _out_kernel(*refs, n_y, final):
    x_ref = refs[0]
    y_refs = refs[1:1 + n_y]
    w_refs = refs[1 + n_y:1 + 2 * n_y]
    p_ref, wp_ref, gn_ref, wg_ref = refs[1 + 2 * n_y:5 + 2 * n_y]
    if final:
        fg_ref, o_ref = refs[5 + 2 * n_y:]
    else:
        (o_ref,) = refs[5 + 2 * n_y:]

    x1 = x_ref[...]
    for y_ref, w_ref in zip(y_refs, w_refs):
        x1 = x1 + jnp.dot(y_ref[...], w_ref[...], preferred_element_type=F32)
    hn = x1 * lax.rsqrt(jnp.mean(x1 * x1, axis=-1, keepdims=True) + RMS_EPS) * gn_ref[...]
    gate = jax.nn.sigmoid(jnp.dot(hn.astype(BF16), wg_ref[...], preferred_element_type=F32))
    pp = jnp.dot(p_ref[...].astype(BF16), wp_ref[...], preferred_element_type=F32)
    x2 = x1 + gate * pp
    if final:
        x2 = x2 * lax.rsqrt(jnp.mean(x2 * x2, axis=-1, keepdims=True) + RMS_EPS) * fg_ref[...]
    o_ref[...] = x2


def _out_block(x, ys, w, p, wp, gn, wg, final_g=None, *, tm):
    s, d = x.shape
    n_y = len(ys)
    final = final_g is not None
    const = lambda i: (0, 0)
    single = pl.Buffered(1)
    in_specs = [pl.BlockSpec((tm, d), lambda i: (i, 0))]
    in_specs += [pl.BlockSpec((tm, y.shape[1]), lambda i: (i, 0)) for y in ys]
    in_specs += [pl.BlockSpec((y.shape[1], d), lambda i, r=r: (r, 0), pipeline_mode=single)
                 for r, y in enumerate(ys)]
    assert all(y.shape[1] == ys[0].shape[1] for y in ys) and len(ys) * ys[0].shape[1] == w.shape[0]
    ws = [w] * n_y
    in_specs += [
        pl.BlockSpec((tm, p.shape[1]), lambda i: (i, 0)),
        pl.BlockSpec(wp.shape, const, pipeline_mode=single),
        pl.BlockSpec((1, d), const),
        pl.BlockSpec(wg.shape, const, pipeline_mode=single),
    ]
    args = [x, *ys, *ws, p, wp, gn.reshape(1, d), wg]
    if final:
        in_specs.append(pl.BlockSpec((1, d), const))
        args.append(final_g.reshape(1, d))
    return pl.pallas_call(
        functools.partial(_out_kernel, n_y=n_y, final=final),
        grid=(s // tm,),
        in_specs=in_specs,
        out_specs=pl.BlockSpec((tm, d), lambda i: (i, 0)),
        out_shape=jax.ShapeDtypeStruct((s, d), F32),
        compiler_params=_cparams(("parallel",)),
    )(*args)


def _alibi_slopes(n):
    return jnp.asarray([2.0 ** (-8.0 * (h + 1) / n) for h in range(n)], dtype=F32)


def kernel(x, p, norm_g, w_in_ab, w_out_ab, attn_sinks, diff_lambda, diff_subln_g,
           w_in_c, w_out_c, forget_bias, ple_proj, ple_gate, ple_norm_g, final_norm_g):
    b, s, d = x.shape
    assert (b, s, d) == (1, SEQ, D_MODEL)
    xs = x.reshape(s, d)
    for i in range(DEPTH):
        j = i // 2
        last = i == DEPTH - 1
        if i % 2 == 0:
            lam_init = 0.8 - 0.6 * math.exp(-0.3 * i)
            tn = 1280
            z = _norm_matmul(xs, norm_g[i], [(w_in_ab[j].astype(BF16), 0, AB_IN // tn)], tm=1024, tn=tn)
            ya = _swa_attention(z, _alibi_slopes(A_HEADS), attn_sinks[j].astype(F32))
            yb = _diff_attention(z, _alibi_slopes(B_HEADS), diff_lambda[j].astype(F32),
                                 diff_subln_g[j].astype(F32), lam_init, tq=1024, tk=512)
            ys, w_out = [ya, yb], w_out_ab[j].astype(BF16)
        else:
            tn = 512
            w = w_in_c[j].astype(BF16)
            w_g = w_in_c[j][:, 3 * C_WIDTH + C_HEADS:].astype(BF16)
            z, f = _norm_matmul(xs, norm_g[i], [(w, 0, 3 * C_WIDTH // tn), (w_g, 0, C_WIDTH // tn)],
                                (w, 3 * C_WIDTH // LANES), tm=1024, tn=tn)
            f_bias = jnp.pad(forget_bias[j].astype(F32), (0, LANES - C_HEADS)).reshape(1, LANES)
            cb = _logsig_cumsum(f, f_bias, C_HEADS // 2)
            ys, w_out = [_fox_attention(z, cb, tq=1024, tk=512)], w_out_c[j].astype(BF16)
        xs = _out_block(xs, ys, w_out, p[i].reshape(s, D_PLE), ple_proj[i].astype(BF16),
                        ple_norm_g[i], ple_gate[i].astype(BF16),
                        final_norm_g if last else None, tm=256)
    return xs.reshape(b, s, d)
```

```python
import functools
import itertools
import math

import jax
import jax.numpy as jnp
from jax import lax
from jax.experimental import pallas as pl
from jax.experimental.pallas import tpu as pltpu

F32 = jnp.float32
BF16 = jnp.bfloat16

D_MODEL = 2048
SEQ = 8192
DEPTH = 2
HEAD_DIM = 64
D_PLE = 256
WINDOW = 128
RMS_EPS = 1e-6
A_HEADS = 16
A_KV_HEADS = 2
A_GROUP = A_HEADS // A_KV_HEADS
A_WIDTH = A_HEADS * HEAD_DIM
B_HEADS = 8
B_WIDTH = B_HEADS * 2 * HEAD_DIM
C_HEADS = 32
C_WIDTH = C_HEADS * HEAD_DIM
AB_IN = 6400
SCALE = HEAD_DIM ** -0.5

LANES = 128
NEG_INF = float("-inf")
VMEM_LIMIT = 48 * 1024 * 1024


def _cparams(sem):
    return pltpu.CompilerParams(dimension_semantics=sem, vmem_limit_bytes=VMEM_LIMIT)


def _silu(g):
    return g * (1.0 / (1.0 + jnp.exp(-g)))


def _norm_matmul_kernel(x_ref, g_ref, *rest, seg_ends, with_extra):
    w_refs, rest = rest[:len(seg_ends)], rest[len(seg_ends):]
    if with_extra:
        we_ref, o_ref, oe_ref, h_ref = rest
    else:
        o_ref, h_ref = rest
    j = pl.program_id(1)

    @pl.when(j == 0)
    def _():
        x = x_ref[...]
        ms = jnp.mean(x * x, axis=-1, keepdims=True)
        h = (x * lax.rsqrt(ms + RMS_EPS) * g_ref[...]).astype(BF16)
        h_ref[...] = h
        if with_extra:
            oe_ref[...] = jnp.dot(h, we_ref[...], preferred_element_type=F32)

    def project(w_ref):
        o_ref[...] = jnp.dot(h_ref[...], w_ref[...], preferred_element_type=F32).astype(o_ref.dtype)

    if len(w_refs) == 1:
        project(w_refs[0])
    else:
        for lo, hi, w_ref in zip((0,) + seg_ends[:-1], seg_ends, w_refs):
            pl.when((j >= lo) & (j < hi))(functools.partial(project, w_ref))


def _norm_matmul(x, g, segments, extra=None, *, tm, tn):
    s, d = x.shape
    seg_ends = tuple(itertools.accumulate(count for _, _, count in segments))
    with_extra = extra is not None
    in_specs = [
        pl.BlockSpec((tm, d), lambda i, j: (i, 0)),
        pl.BlockSpec((1, d), lambda i, j: (0, 0)),
    ]
    args = [x, g.reshape(1, d)]
    for (w, first, count), end in zip(segments, seg_ends):
        in_specs.append(pl.BlockSpec(
            (d, tn), lambda i, j, first=first, count=count, lo=end - count:
            (0, first + jnp.clip(j - lo, 0, count - 1))))
        args.append(w)
    n = seg_ends[-1] * tn
    out_shape = [jax.ShapeDtypeStruct((s, n), BF16)]
    out_specs = [pl.BlockSpec((tm, tn), lambda i, j: (i, j))]
    if with_extra:
        we, tile = extra
        in_specs.append(pl.BlockSpec((d, LANES), lambda i, j: (0, tile)))
        out_shape.append(jax.ShapeDtypeStruct((s, LANES), F32))
        out_specs.append(pl.BlockSpec((tm, LANES), lambda i, j: (i, 0)))
        args.append(we)
    res = pl.pallas_call(
        functools.partial(_norm_matmul_kernel, seg_ends=seg_ends, with_extra=with_extra),
        grid=(s // tm, n // tn),
        in_specs=in_specs,
        out_specs=out_specs,
        out_shape=out_shape,
        scratch_shapes=[pltpu.VMEM((tm, d), BF16)],
        compiler_params=_cparams(("parallel", "arbitrary")),
    )(*args)
    return res if with_extra else res[0]


N_SPLIT = 3


def _split_bf16(x):
    parts = []
    for _ in range(N_SPLIT):
        t = x.astype(BF16)
        parts.append(t)
        x = x - t.astype(F32)
    return parts


def _logsig_cumsum_kernel(f_ref, b_ref, o_ref, carry_ref, sel_ref, *, blk):
    n = f_ref.shape[1]

    @pl.when(pl.program_id(0) == 0)
    def _():
        carry_ref[...] = jnp.zeros_like(carry_ref)
        r = lax.broadcasted_iota(jnp.int32, sel_ref.shape, 0)
        l = lax.broadcasted_iota(jnp.int32, sel_ref.shape, 1)
        part, head = r // n, r % n
        lane = jnp.where(l // LANES == head // 2, l % LANES, -1)
        sel_ref[...] = jnp.where(lane == N_SPLIT * (head % 2) + part, 1.0, 0.0).astype(BF16)

    z = f_ref[...] + b_ref[...]
    ls = jnp.minimum(z, 0.0) - jnp.log1p(jnp.exp(-jnp.abs(z)))
    row = lax.broadcasted_iota(jnp.int32, (blk, blk), 0)
    col = lax.broadcasted_iota(jnp.int32, (blk, blk), 1)
    lower = (col <= row).astype(F32)
    c = jnp.dot(lower, ls, preferred_element_type=F32,
                precision=lax.Precision.HIGHEST) + carry_ref[...]
    carry_ref[...] = c[blk - 1:blk, :]
    terms = jnp.concatenate(_split_bf16(c), axis=1)
    o_ref[...] = jnp.dot(terms, sel_ref[...], preferred_element_type=F32).astype(BF16)


def _logsig_cumsum(f, bias_row, n_pairs, *, blk=256):
    s, n = f.shape
    return pl.pallas_call(
        functools.partial(_logsig_cumsum_kernel, blk=blk),
        grid=(s // blk,),
        in_specs=[pl.BlockSpec((blk, n), lambda i: (i, 0)),
                  pl.BlockSpec((1, n), lambda i: (0, 0))],
        out_specs=pl.BlockSpec((blk, n_pairs * LANES), lambda i: (i, 0)),
        out_shape=jax.ShapeDtypeStruct((s, n_pairs * LANES), BF16),
        scratch_shapes=[pltpu.VMEM((1, n), F32), pltpu.VMEM((N_SPLIT * n, n_pairs * LANES), BF16)],
        compiler_params=_cparams(("arbitrary",)),
    )(f, bias_row)


ONES_ROWS = 16
CHUNK = 256
LAG_LIMIT = 30.0
LOG_SKIP = 100.0


def _flash_scratch(s, tq, tk, rows):
    return [
        pltpu.VMEM((s // tk, LANES + ONES_ROWS, tk), BF16),
        pltpu.VMEM((2, 2 * LANES, tq), BF16),
        pltpu.VMEM((2, 1, tq), F32),
        pltpu.VMEM((2, rows + ONES_ROWS, tq), F32),
        pltpu.VMEM((2, 1, tq), F32),
        pltpu.VMEM((2, 1, tq), F32),
        pltpu.VMEM((2, s // tk, 1), F32),
        *[pltpu.VMEM((tk, tq), F32) for _ in range(4)],
        *[pltpu.VMEM((1, tq), F32) for _ in range(4)],
        *[pltpu.VMEM((tk, tq), BF16) for _ in range(4)],
        *[pltpu.VMEM((1, tq), F32) for _ in range(4)],
    ]


def _fill_vt(v_ref, vt_ref, tk):
    def body(c, carry):
        vt_ref[c, :LANES, :] = v_ref[pl.ds(pl.multiple_of(c * tk, tk), tk), :].T
        vt_ref[c, LANES:, :] = jnp.ones((ONES_ROWS, tk), BF16)
        return carry
    lax.fori_loop(0, vt_ref.shape[0], body, 0)


def _map_lanes(m):
    lane = lax.broadcasted_iota(jnp.int32, (1, LANES), 1)
    return (lane < HEAD_DIM) if m == 0 else (lane >= HEAD_DIM)


def _max_sq_norm(x, m):
    x = x.astype(F32)
    rows = jnp.sum(jnp.where(_map_lanes(m), x * x, 0.0), axis=1, keepdims=True)
    return jnp.max(rows, axis=0, keepdims=True)


def _fill_key_norms(k_ref, kn_ref, tk):
    for m in range(2):
        kn_ref[m] = jnp.concatenate([_max_sq_norm(k_ref[j * tk:(j + 1) * tk, :], m)
                                     for j in range(kn_ref.shape[1])], axis=0)


def _pair_flash(q_ref, k_ref, kb_fn, off_fn, decay_fn, i, scratch, *, tq, tk, v_rows, bias_rows, bias_sign):
    assert tq == 2 * tk
    vt_ref, qt_ref, m_ref, acc_ref, g_ref, bad_ref, kn_ref = scratch[:7]
    s_ref, mt_ref, p_ref, a_ref = ({(slot, h): scratch[7 + 4 * n + 2 * slot + h]
                                    for slot in range(2) for h in range(2)} for n in range(4))
    row = lax.broadcasted_iota(jnp.int32, (LANES, 1), 0)
    lo = row < HEAD_DIM
    qt = (q_ref[...] * jnp.asarray(SCALE, BF16)).T
    zero = jnp.zeros_like(qt)
    for h in range(2):
        qt_ref[h, :LANES, :] = jnp.where(lo, qt, zero) if h == 0 else jnp.where(lo, zero, qt)
        b0, b1 = bias_rows[h]
        qt_ref[h, LANES:, :] = jnp.where((row >= b0) & (row < b1), bias_sign, 0.0).astype(BF16) + zero

    lanes = [(h, slice(c, c + CHUNK)) for c in range(0, tq, CHUNK) for h in range(2)]

    def tile_of(n):
        return jnp.where(n < 2, 2 * i + n, n - 2)

    def raw_scores(j, diag, h, cs):
        j0 = pl.multiple_of(j * tk, tk)
        lhs = jnp.concatenate([k_ref[pl.ds(j0, tk), :], kb_fn(j0)], axis=1)
        t = jnp.dot(lhs, qt_ref[h, :, cs], preferred_element_type=F32)
        if diag is not None:
            kpos = lax.broadcasted_iota(jnp.int32, t.shape, 0) + diag * tk
            qpos = lax.broadcasted_iota(jnp.int32, t.shape, 1) + cs.start
            t = jnp.where(kpos <= qpos, t, NEG_INF)
        return t

    def scores(j, slot, diag, h, cs):
        t = raw_scores(j, diag, h, cs)
        s_ref[slot, h][:, cs] = t
        mt_ref[slot, h][:, cs] = jnp.max(t, axis=0, keepdims=True)

    def values(j, slot, h, cs, first=False):
        r0, r1 = v_rows[h]
        vt = vt_ref[j, r0:r1, :]
        if r1 != LANES + ONES_ROWS:
            vt = jnp.concatenate([vt, vt_ref[j, LANES:, :]], axis=0)
        pv = jnp.dot(vt, p_ref[slot, h][:, cs], preferred_element_type=F32)
        acc_ref[h, :, cs] = pv if first else acc_ref[h, :, cs] * a_ref[slot, h][:, cs] + pv

    def softmax(j, slot, h, cs):
        off = off_fn(j)
        mt = mt_ref[slot, h][:, cs]
        m_old = m_ref[h, :, cs]
        m_new = jnp.maximum(m_old, mt if off is None else mt + off)
        a_ref[1 - slot, h][:, cs] = jnp.exp(m_old - m_new)
        m_ref[h, :, cs] = m_new
        shift = m_new if off is None else m_new - off
        p_ref[1 - slot, h][:, cs] = jnp.exp(s_ref[slot, h][:, cs] - shift).astype(BF16)

    def exact_step(t, slot, diag=None, first=False, last=False):
        for h, cs in lanes:
            if not first:
                values(tile_of(t - 1), slot, h, cs)
            softmax(tile_of(t), slot, h, cs)
            if not last:
                scores(tile_of(t + 1), 1 - slot, diag, h, cs)

    def exact_sweep():
        for h in range(2):
            m_ref[h] = jnp.full((1, tq), NEG_INF, F32)
            acc_ref[h] = jnp.zeros(acc_ref.shape[1:], F32)
        for h, cs in lanes:
            scores(2 * i, 0, 0, h, cs)
        exact_step(0, 0, 1, first=True)

        def two_steps(u, carry):
            exact_step(2 * u + 1, 1)
            exact_step(2 * u + 2, 0)
            return carry

        lax.fori_loop(0, i, two_steps, 0)
        exact_step(2 * i + 1, 1, last=True)
        for h, cs in lanes:
            values(tile_of(2 * i + 1), 0, h, cs)

    def fused(j, slot, diag, h, cs):
        off = off_fn(j)
        t = raw_scores(j, diag, h, cs)
        ref = m_ref[h, :, cs]
        p_ref[slot, h][:, cs] = jnp.exp(t - (ref if off is None else ref - off)).astype(BF16)
        mt = jnp.max(t, axis=0, keepdims=True)
        mt = mt if off is None else mt + off
        a_ref[slot, h][:, cs] = g_ref[h, :, cs]
        m_new = jnp.maximum(ref, mt)
        g_ref[h, :, cs] = jnp.exp(ref - m_new)
        m_ref[h, :, cs] = m_new
        bad_ref[h, :, cs] = jnp.maximum(bad_ref[h, :, cs], mt - ref)

    def first_needed_tile():
        n_tiles = kn_ref.shape[1]
        jt = lax.broadcasted_iota(jnp.int32, (n_tiles, 1), 0)
        needed = jnp.zeros((n_tiles, 1), F32)
        for m in range(2):
            qn = jnp.sqrt(_max_sq_norm(q_ref[...], m)) * SCALE
            kn = jnp.sqrt(kn_ref[m])
            own = jnp.max(jnp.where((jt == 2 * i) | (jt == 2 * i + 1), kn, 0.0), axis=0, keepdims=True)
            needed = jnp.maximum(needed, jnp.where(decay_fn(m) <= LOG_SKIP + qn * (kn + own), 1.0, 0.0))
        first = jnp.min(jnp.where((needed > 0.0) & (jt < 2 * i), jt, 2 * i).astype(F32))
        return first.astype(jnp.int32)

    def lagged_step(prev, cur, slot):
        for h, cs in lanes:
            values(prev, 1 - slot, h, cs)
            fused(cur, slot, None, h, cs)

    def lagged_sweep():
        d0, d1 = 2 * i, 2 * i + 1
        high = lambda cs: cs.stop > tk
        for h, cs in lanes:
            scores(d1 if high(cs) else d0, 0, 1 if high(cs) else 0, h, cs)
        for h, cs in lanes:
            off = off_fn(d1 if high(cs) else d0)
            mt = mt_ref[0, h][:, cs]
            p_ref[1, h][:, cs] = jnp.exp(s_ref[0, h][:, cs] - mt).astype(BF16)
            m_ref[h, :, cs] = mt if off is None else mt + off
            g_ref[h, :, cs] = jnp.ones((1, CHUNK), F32)
            bad_ref[h, :, cs] = jnp.full((1, CHUNK), NEG_INF, F32)
        for h, cs in lanes:
            values(d1 if high(cs) else d0, 1, h, cs, first=True)
            if high(cs):
                fused(d0, 0, None, h, cs)
            else:
                p_ref[0, h][:, cs] = jnp.zeros((tk, CHUNK), BF16)
                a_ref[0, h][:, cs] = jnp.ones((1, CHUNK), F32)

        def two_steps(u, prev):
            lagged_step(prev, 2 * u, 1)
            lagged_step(2 * u, 2 * u + 1, 0)
            return 2 * u + 1

        prev = lax.fori_loop(first_needed_tile() // 2, i, two_steps, d0)
        for h, cs in lanes:
            values(prev, 0, h, cs)

    lagged_sweep()
    worst = jnp.max(bad_ref[...])

    @pl.when(jnp.logical_not(worst <= LAG_LIMIT))
    def _():
        exact_sweep()

    return acc_ref


def _fox_kernel(q_ref, k_ref, v_ref, g_ref, cb_ref, o_ref, ce_ref, *scratch, tq, tk):
    i = pl.program_id(1)
    lane = lax.broadcasted_iota(jnp.int32, (1, LANES), 1)
    pack = 16

    def c_at(rows, last, m):
        terms = cb_ref[rows, :].astype(F32)[pack - 1:pack, :] if last else cb_ref[rows, :].astype(F32)[0:1, :]
        mine = (lane >= N_SPLIT * m) & (lane < N_SPLIT * (m + 1))
        return jnp.sum(jnp.where(mine, terms, 0.0), axis=1, keepdims=True)

    @pl.when(i == 0)
    def _():
        _fill_vt(v_ref, scratch[0], tk)
        _fill_key_norms(k_ref, scratch[6], tk)
        for m in range(2):
            ce_ref[m] = jnp.concatenate([c_at(slice((j + 1) * tk - pack, (j + 1) * tk), True, m)
                                         for j in range(ce_ref.shape[1])], axis=0)

    def decay_fn(m):
        return ce_ref[m] - c_at(pl.ds(pl.multiple_of(i * tq, tq), pack), False, m)

    acc_ref = _pair_flash(q_ref, k_ref, lambda j0: cb_ref[pl.ds(j0, tk), :], lambda j: None, decay_fn, i, scratch,
                          tq=tq, tk=tk, v_rows=((0, HEAD_DIM), (HEAD_DIM, LANES)),
                          bias_rows=((0, N_SPLIT), (N_SPLIT, 2 * N_SPLIT)), bias_sign=-1.0)
    d = HEAD_DIM
    y = jnp.concatenate([acc_ref[0, :d] / acc_ref[0, d:d + 1], acc_ref[1, :d] / acc_ref[1, d:d + 1]], axis=0).T
    o_ref[...] = (y * _silu(g_ref[...].astype(F32))).astype(o_ref.dtype)


def _fox_attention(z, cb, *, tq, tk):
    s = z.shape[0]
    nb = C_WIDTH // LANES
    return pl.pallas_call(
        functools.partial(_fox_kernel, tq=tq, tk=tk),
        grid=(nb, s // tq),
        in_specs=[
            pl.BlockSpec((tq, LANES), lambda p, i: (i, p)),
            pl.BlockSpec((s, LANES), lambda p, i: (0, nb + p)),
            pl.BlockSpec((s, LANES), lambda p, i: (0, 2 * nb + p)),
            pl.BlockSpec((tq, LANES), lambda p, i: (i, 3 * nb + p)),
            pl.BlockSpec((s, LANES), lambda p, i: (0, p)),
        ],
        out_specs=pl.BlockSpec((tq, LANES), lambda p, i: (i, p)),
        out_shape=jax.ShapeDtypeStruct((s, C_WIDTH), BF16),
        scratch_shapes=[pltpu.VMEM((2, s // tk, 1), F32)] + _flash_scratch(s, tq, tk, HEAD_DIM),
        compiler_params=_cparams(("parallel", "arbitrary")),
    )(z, z, z, z, cb)


def _diff_kernel(q_ref, k_ref, v_ref, g_ref, slope_ref, lam_ref, sg_ref, o_ref, kb_ref, *scratch,
                 tq, tk, lam_init):
    i = pl.program_id(1)
    slope = slope_ref[...]

    @pl.when(i == 0)
    def _():
        _fill_vt(v_ref, scratch[0], tk)
        _fill_key_norms(k_ref, scratch[6], tk)
        r = lax.broadcasted_iota(jnp.int32, (tk, LANES), 0).astype(F32)
        lane = lax.broadcasted_iota(jnp.int32, (tk, LANES), 1)
        terms = _split_bf16(slope * r)
        kb = jnp.zeros((tk, LANES), F32)
        for n, t in enumerate(terms):
            kb = jnp.where(lane == n, t.astype(F32), kb)
        kb_ref[...] = kb.astype(BF16)

    def off_fn(j):
        return slope * (j * tk - i * tq).astype(F32)

    def decay_fn(m):
        ends = lax.broadcasted_iota(jnp.int32, (scratch[6].shape[1], 1), 0) * tk + (tk - 1)
        return slope * (i * tq - ends).astype(F32)

    acc_ref = _pair_flash(q_ref, k_ref, lambda j0: kb_ref[...], off_fn, decay_fn, i, scratch,
                          tq=tq, tk=tk, v_rows=((0, LANES + ONES_ROWS), (0, LANES + ONES_ROWS)),
                          bias_rows=((0, N_SPLIT), (0, N_SPLIT)), bias_sign=1.0)
    lp = lam_ref[...]
    lam = (jnp.exp(jnp.sum(lp[0:1] * lp[1:2], axis=-1, keepdims=True))
           - jnp.exp(jnp.sum(lp[2:3] * lp[3:4], axis=-1, keepdims=True)) + lam_init)
    d = LANES
    o = (acc_ref[0, :d] / acc_ref[0, d:d + 1] - lam * (acc_ref[1, :d] / acc_ref[1, d:d + 1])).T
    o = o * lax.rsqrt(jnp.mean(o * o, axis=-1, keepdims=True) + RMS_EPS) * sg_ref[...]
    o = o * (1.0 - lam_init)
    o_ref[...] = (o * _silu(g_ref[...].astype(F32))).astype(o_ref.dtype)


def _diff_attention(z, slopes, lam_params, subln_g, lam_init, *, tq, tk):
    s = z.shape[0]
    qb, kb, vb, gb = 18, 26, 34, 42
    return pl.pallas_call(
        functools.partial(_diff_kernel, tq=tq, tk=tk, lam_init=lam_init),
        grid=(B_HEADS, s // tq),
        in_specs=[
            pl.BlockSpec((tq, LANES), lambda h, i: (i, qb + h)),
            pl.BlockSpec((s, LANES), lambda h, i: (0, kb + h)),
            pl.BlockSpec((s, LANES), lambda h, i: (0, vb + h)),
            pl.BlockSpec((tq, LANES), lambda h, i: (i, gb + h)),
            pl.BlockSpec((None, 1, 1), lambda h, i: (h, 0, 0)),
            pl.BlockSpec((4, HEAD_DIM), lambda h, i: (0, 0)),
            pl.BlockSpec((1, LANES), lambda h, i: (0, 0)),
        ],
        out_specs=pl.BlockSpec((tq, LANES), lambda h, i: (i, h)),
        out_shape=jax.ShapeDtypeStruct((s, B_WIDTH), BF16),
        scratch_shapes=[pltpu.VMEM((tk, LANES), BF16)] + _flash_scratch(s, tq, tk, LANES),
        compiler_params=_cparams(("parallel", "arbitrary")),
    )(z, z, z, z, slopes.reshape(B_HEADS, 1, 1), lam_params, subln_g.reshape(1, LANES))


def _swa_kernel(slope_ref, sink_ref, q_ref, kp_ref, kc_ref, vp_ref, vc_ref, *rest):
    *g_refs, o_ref = rest
    i = pl.program_id(0)
    blk = WINDOW
    lane = lax.broadcasted_iota(jnp.int32, (1, LANES), 1)
    lo = lane < HEAD_DIM

    def swap_halves(a):
        return pltpu.roll(a.astype(F32), HEAD_DIM, 1).astype(BF16)

    k = jnp.concatenate([kp_ref[...], kc_ref[...]], axis=0)
    v = jnp.concatenate([vp_ref[...], vc_ref[...]], axis=0)
    k_sw, v_sw = swap_halves(k), swap_halves(v)

    r = lax.broadcasted_iota(jnp.int32, (blk, 2 * blk), 0)
    c = lax.broadcasted_iota(jnp.int32, (blk, 2 * blk), 1)
    dist = r - c + blk
    valid = (dist >= 0) & (dist < WINDOW) & ((c >= blk) | (i > 0))
    neg_dist = jnp.where(valid, -dist.astype(F32), NEG_INF)

    for pair in range(A_HEADS // 2):
        qp = q_ref[:, pair * LANES:(pair + 1) * LANES] * jnp.asarray(SCALE, BF16)
        zero = jnp.zeros_like(qp)
        outs = []
        for a in range(2):
            h = 2 * pair + a
            kv = h // A_GROUP
            qh = jnp.where(lo, qp, zero) if a == 0 else jnp.where(lo, zero, qp)
            kk, vv = (k, v) if a == kv else (k_sw, v_sw)
            s = lax.dot_general(qh, kk, (((1,), (1,)), ((), ())), preferred_element_type=F32)
            s = s + slope_ref[h] * neg_dist
            sink = sink_ref[h]
            m = jnp.maximum(jnp.max(s, axis=-1, keepdims=True), sink)
            e = jnp.exp(s - m)
            denom = jnp.sum(e, axis=-1, keepdims=True) + jnp.exp(sink - m)
            outs.append(jnp.dot(e.astype(BF16), vv, preferred_element_type=F32) / denom)
        y = jnp.where(lo, outs[0], outs[1])
        g_ref = g_refs[pair // 2]
        g = g_ref[:, (pair % 2) * LANES:(pair % 2 + 1) * LANES].astype(F32)
        o_ref[:, pair * LANES:(pair + 1) * LANES] = (y * _silu(g)).astype(o_ref.dtype)


def _swa_attention(z, slopes, sinks):
    s = z.shape[0]
    blk = WINDOW
    nq = A_WIDTH // LANES
    kb, vb = nq, nq + 1
    gw = 2 * LANES
    g_first = (nq + 2) * LANES // gw
    assert g_first * gw == (nq + 2) * LANES
    smem = pl.BlockSpec(memory_space=pltpu.SMEM)
    prev = lambda i: (jnp.maximum(i - 1, 0), kb)
    prev_v = lambda i: (jnp.maximum(i - 1, 0), vb)
    return pl.pallas_call(
        _swa_kernel,
        grid=(s // blk,),
        in_specs=[
            smem, smem,
            pl.BlockSpec((blk, A_WIDTH), lambda i: (i, 0)),
            pl.BlockSpec((blk, LANES), prev),
            pl.BlockSpec((blk, LANES), lambda i: (i, kb)),
            pl.BlockSpec((blk, LANES), prev_v),
            pl.BlockSpec((blk, LANES), lambda i: (i, vb)),
            *[pl.BlockSpec((blk, gw), lambda i, c=g_first + n: (i, c)) for n in range(A_WIDTH // gw)],
        ],
        out_specs=pl.BlockSpec((blk, A_WIDTH), lambda i: (i, 0)),
        out_shape=jax.ShapeDtypeStruct((s, A_WIDTH), BF16),
        compiler_params=_cparams(("parallel",)),
    )(slopes, sinks, *([z] * (5 + A_WIDTH // gw)))


def _out_kernel(*refs, n_y, final):
    x_ref = refs[0]
    y_refs = refs[1:1 + n_y]
    w_refs = refs[1 + n_y:1 + 2 * n_y]
    p_ref, wp_ref, gn_ref, wg_ref = refs[1 + 2 * n_y:5 + 2 * n_y]
    if final:
        fg_ref, o_ref = refs[5 + 2 * n_y:]
    else:
        (o_ref,) = refs[5 + 2 * n_y:]

    x1 = x_ref[...]
    for y_ref, w_ref in zip(y_refs, w_refs):
        x1 = x1 + jnp.dot(y_ref[...], w_ref[...], preferred_element_type=F32)
    hn = x1 * lax.rsqrt(jnp.mean(x1 * x1, axis=-1, keepdims=True) + RMS_EPS) * gn_ref[...]
    gate = jax.nn.sigmoid(jnp.dot(hn.astype(BF16), wg_ref[...], preferred_element_type=F32))
    pp = jnp.dot(p_ref[...].astype(BF16), wp_ref[...], preferred_element_type=F32)
    x2 = x1 + gate * pp
    if final:
        x2 = x2 * lax.rsqrt(jnp.mean(x2 * x2, axis=-1, keepdims=True) + RMS_EPS) * fg_ref[...]
    o_ref[...] = x2


def _out_block(x, ys, w, p, wp, gn, wg, final_g=None, *, tm):
    s, d = x.shape
    n_y = len(ys)
    final = final_g is not None
    const = lambda i: (0, 0)
    single = pl.Buffered(1)
    in_specs = [pl.BlockSpec((tm, d), lambda i: (i, 0))]
    in_specs += [pl.BlockSpec((tm, y.shape[1]), lambda i: (i, 0)) for y in ys]
    in_specs += [pl.BlockSpec((y.shape[1], d), lambda i, r=r: (r, 0), pipeline_mode=single)
                 for r, y in enumerate(ys)]
    assert all(y.shape[1] == ys[0].shape[1] for y in ys) and len(ys) * ys[0].shape[1] == w.shape[0]
    ws = [w] * n_y
    in_specs += [
        pl.BlockSpec((tm, p.shape[1]), lambda i: (i, 0)),
        pl.BlockSpec(wp.shape, const, pipeline_mode=single),
        pl.BlockSpec((1, d), const),
        pl.BlockSpec(wg.shape, const, pipeline_mode=single),
    ]
    args = [x, *ys, *ws, p, wp, gn.reshape(1, d), wg]
    if final:
        in_specs.append(pl.BlockSpec((1, d), const))
        args.append(final_g.reshape(1, d))
    return pl.pallas_call(
        functools.partial(_out_kernel, n_y=n_y, final=final),
        grid=(s // tm,),
        in_specs=in_specs,
        out_specs=pl.BlockSpec((tm, d), lambda i: (i, 0)),
        out_shape=jax.ShapeDtypeStruct((s, d), F32),
        compiler_params=_cparams(("parallel",)),
    )(*args)


def _alibi_slopes(n):
    return jnp.asarray([2.0 ** (-8.0 * (h + 1) / n) for h in range(n)], dtype=F32)


def kernel(x, p, norm_g, w_in_ab, w_out_ab, attn_sinks, diff_lambda, diff_subln_g,
           w_in_c, w_out_c, forget_bias, ple_proj, ple_gate, ple_norm_g, final_norm_g):
    b, s, d = x.shape
    assert (b, s, d) == (1, SEQ, D_MODEL)
    xs = x.reshape(s, d)
    for i in range(DEPTH):
        j = i // 2
        last = i == DEPTH - 1
        if i % 2 == 0:
            lam_init = 0.8 - 0.6 * math.exp(-0.3 * i)
            tn = 1280
            z = _norm_matmul(xs, norm_g[i], [(w_in_ab[j].astype(BF16), 0, AB_IN // tn)], tm=1024, tn=tn)
            ya = _swa_attention(z, _alibi_slopes(A_HEADS), attn_sinks[j].astype(F32))
            yb = _diff_attention(z, _alibi_slopes(B_HEADS), diff_lambda[j].astype(F32),
                                 diff_subln_g[j].astype(F32), lam_init, tq=1024, tk=512)
            ys, w_out = [ya, yb], w_out_ab[j].astype(BF16)
        else:
            tn = 512
            w = w_in_c[j].astype(BF16)
            w_g = w_in_c[j][:, 3 * C_WIDTH + C_HEADS:].astype(BF16)
            z, f = _norm_matmul(xs, norm_g[i], [(w, 0, 3 * C_WIDTH // tn), (w_g, 0, C_WIDTH // tn)],
                                (w, 3 * C_WIDTH // LANES), tm=1024, tn=tn)
            f_bias = jnp.pad(forget_bias[j].astype(F32), (0, LANES - C_HEADS)).reshape(1, LANES)
            cb = _logsig_cumsum(f, f_bias, C_HEADS // 2)
            ys, w_out = [_fox_attention(z, cb, tq=1024, tk=512)], w_out_c[j].astype(BF16)
        xs = _out_block(xs, ys, w_out, p[i].reshape(s, D_PLE), ple_proj[i].astype(BF16),
                        ple_norm_g[i], ple_gate[i].astype(BF16),
                        final_norm_g if last else None, tm=256)
    return xs.reshape(b, s, d)
```

```python
import functools
import itertools
import math

import jax
import jax.numpy as jnp
from jax import lax
from jax.experimental import pallas as pl
from jax.experimental.pallas import tpu as pltpu

F32 = jnp.float32
BF16 = jnp.bfloat16

D_MODEL = 2048
SEQ = 8192
DEPTH = 2
HEAD_DIM = 64
D_PLE = 256
WINDOW = 128
RMS_EPS = 1e-6
A_HEADS = 16
A_KV_HEADS = 2
A_GROUP = A_HEADS // A_KV_HEADS
A_WIDTH = A_HEADS * HEAD_DIM
B_HEADS = 8
B_WIDTH = B_HEADS * 2 * HEAD_DIM
C_HEADS = 32
C_WIDTH = C_HEADS * HEAD_DIM
AB_IN = 6400
SCALE = HEAD_DIM ** -0.5

LANES = 128
NEG_INF = float("-inf")
VMEM_LIMIT = 48 * 1024 * 1024


def _cparams(sem):
    return pltpu.CompilerParams(dimension_semantics=sem, vmem_limit_bytes=VMEM_LIMIT)


def _silu(g):
    return g * (1.0 / (1.0 + jnp.exp(-g)))


NORM_ROWS = 256


def _norm_matmul_kernel(x_ref, g_ref, *rest, seg_ends, with_extra):
    w_refs, rest = rest[:len(seg_ends)], rest[len(seg_ends):]
    if with_extra:
        we_ref, o_ref, oe_ref, h_ref = rest
    else:
        o_ref, h_ref = rest
    j = pl.program_id(1)

    @pl.when(j == 0)
    def _():
        def norm_rows(r, carry):
            rows = pl.ds(pl.multiple_of(r * NORM_ROWS, NORM_ROWS), NORM_ROWS)
            x = x_ref[rows, :]
            ms = jnp.mean(x * x, axis=-1, keepdims=True)
            h_ref[rows, :] = (x * lax.rsqrt(ms + RMS_EPS) * g_ref[...]).astype(BF16)
            return carry
        lax.fori_loop(0, x_ref.shape[0] // NORM_ROWS, norm_rows, 0)
        if with_extra:
            oe_ref[...] = jnp.dot(h_ref[...], we_ref[...], preferred_element_type=F32)

    def project(w_ref):
        o_ref[...] = jnp.dot(h_ref[...], w_ref[...], preferred_element_type=F32).astype(o_ref.dtype)

    if len(w_refs) == 1:
        project(w_refs[0])
    else:
        for lo, hi, w_ref in zip((0,) + seg_ends[:-1], seg_ends, w_refs):
            pl.when((j >= lo) & (j < hi))(functools.partial(project, w_ref))


def _norm_matmul(x, g, segments, extra=None, *, tm, tn):
    s, d = x.shape
    seg_ends = tuple(itertools.accumulate(count for _, _, count in segments))
    with_extra = extra is not None
    in_specs = [
        pl.BlockSpec((tm, d), lambda i, j: (i, 0)),
        pl.BlockSpec((1, d), lambda i, j: (0, 0)),
    ]
    args = [x, g.reshape(1, d)]
    for (w, first, count), end in zip(segments, seg_ends):
        in_specs.append(pl.BlockSpec(
            (d, tn), lambda i, j, first=first, count=count, lo=end - count:
            (0, first + jnp.clip(j - lo, 0, count - 1))))
        args.append(w)
    n = seg_ends[-1] * tn
    out_shape = [jax.ShapeDtypeStruct((s, n), BF16)]
    out_specs = [pl.BlockSpec((tm, tn), lambda i, j: (i, j))]
    if with_extra:
        we, tile = extra
        in_specs.append(pl.BlockSpec((d, LANES), lambda i, j: (0, tile)))
        out_shape.append(jax.ShapeDtypeStruct((s, LANES), F32))
        out_specs.append(pl.BlockSpec((tm, LANES), lambda i, j: (i, 0)))
        args.append(we)
    res = pl.pallas_call(
        functools.partial(_norm_matmul_kernel, seg_ends=seg_ends, with_extra=with_extra),
        grid=(s // tm, n // tn),
        in_specs=in_specs,
        out_specs=out_specs,
        out_shape=out_shape,
        scratch_shapes=[pltpu.VMEM((tm, d), BF16)],
        compiler_params=_cparams(("parallel", "arbitrary")),
    )(*args)
    return res if with_extra else res[0]


N_SPLIT = 3


def _split_bf16(x):
    parts = []
    for _ in range(N_SPLIT):
        t = x.astype(BF16)
        parts.append(t)
        x = x - t.astype(F32)
    return parts


def _logsig_cumsum_kernel(f_ref, b_ref, o_ref, carry_ref, sel_ref, *, blk):
    n = f_ref.shape[1]

    @pl.when(pl.program_id(0) == 0)
    def _():
        carry_ref[...] = jnp.zeros_like(carry_ref)
        r = lax.broadcasted_iota(jnp.int32, sel_ref.shape, 0)
        l = lax.broadcasted_iota(jnp.int32, sel_ref.shape, 1)
        part, head = r // n, r % n
        lane = jnp.where(l // LANES == head // 2, l % LANES, -1)
        sel_ref[...] = jnp.where(lane == N_SPLIT * (head % 2) + part, 1.0, 0.0).astype(BF16)

    z = f_ref[...] + b_ref[...]
    ls = jnp.minimum(z, 0.0) - jnp.log1p(jnp.exp(-jnp.abs(z)))
    row = lax.broadcasted_iota(jnp.int32, (blk, blk), 0)
    col = lax.broadcasted_iota(jnp.int32, (blk, blk), 1)
    lower = (col <= row).astype(F32)
    c = jnp.dot(lower, ls, preferred_element_type=F32,
                precision=lax.Precision.HIGHEST) + carry_ref[...]
    carry_ref[...] = c[blk - 1:blk, :]
    terms = jnp.concatenate(_split_bf16(c), axis=1)
    o_ref[...] = jnp.dot(terms, sel_ref[...], preferred_element_type=F32).astype(BF16)


def _logsig_cumsum(f, bias_row, n_pairs, *, blk=256):
    s, n = f.shape
    return pl.pallas_call(
        functools.partial(_logsig_cumsum_kernel, blk=blk),
        grid=(s // blk,),
        in_specs=[pl.BlockSpec((blk, n), lambda i: (i, 0)),
                  pl.BlockSpec((1, n), lambda i: (0, 0))],
        out_specs=pl.BlockSpec((blk, n_pairs * LANES), lambda i: (i, 0)),
        out_shape=jax.ShapeDtypeStruct((s, n_pairs * LANES), BF16),
        scratch_shapes=[pltpu.VMEM((1, n), F32), pltpu.VMEM((N_SPLIT * n, n_pairs * LANES), BF16)],
        compiler_params=_cparams(("arbitrary",)),
    )(f, bias_row)


ONES_ROWS = 16
CHUNK = 256
LAG_LIMIT = 30.0
LOG_SKIP = 94.0


def _flash_scratch(s, tq, tk, rows):
    return [
        pltpu.VMEM((s // tk, LANES + ONES_ROWS, tk), BF16),
        pltpu.VMEM((2, 2 * LANES, tq), BF16),
        pltpu.VMEM((2, 1, tq), F32),
        pltpu.VMEM((2, rows + ONES_ROWS, tq), F32),
        pltpu.VMEM((2, 1, tq), F32),
        pltpu.VMEM((2, 1, tq), F32),
        pltpu.VMEM((2, s // tk, 1), F32),
        *[pltpu.VMEM((tk, tq), F32) for _ in range(4)],
        *[pltpu.VMEM((1, tq), F32) for _ in range(4)],
        *[pltpu.VMEM((tk, tq), BF16) for _ in range(4)],
        *[pltpu.VMEM((1, tq), F32) for _ in range(4)],
    ]


def _fill_vt(v_ref, vt_ref, tk):
    def body(c, carry):
        vt_ref[c, :LANES, :] = v_ref[pl.ds(pl.multiple_of(c * tk, tk), tk), :].T
        vt_ref[c, LANES:, :] = jnp.ones((ONES_ROWS, tk), BF16)
        return carry
    lax.fori_loop(0, vt_ref.shape[0], body, 0)


def _map_lanes(m):
    lane = lax.broadcasted_iota(jnp.int32, (1, LANES), 1)
    return (lane < HEAD_DIM) if m == 0 else (lane >= HEAD_DIM)


def _max_sq_norm(x, m):
    x = x.astype(F32)
    rows = jnp.sum(jnp.where(_map_lanes(m), x * x, 0.0), axis=1, keepdims=True)
    return jnp.max(rows, axis=0, keepdims=True)


def _fill_key_norms(k_ref, kn_ref, tk):
    r = lax.broadcasted_iota(jnp.int32, (LANES, LANES), 0)
    c = lax.broadcasted_iota(jnp.int32, (LANES, LANES), 1)
    sel = jnp.where(c == jnp.where(r < HEAD_DIM, 0, 1), 1.0, 0.0).astype(BF16)
    tiles = []
    for j in range(kn_ref.shape[1]):
        kt = k_ref[j * tk:(j + 1) * tk, :].astype(F32)
        sq = kt * kt
        hi = sq.astype(BF16)
        lo = (sq - hi.astype(F32)).astype(BF16)
        sums = (jnp.dot(hi, sel, preferred_element_type=F32) + jnp.dot(lo, sel, preferred_element_type=F32))
        tiles.append(jnp.max(sums, axis=0, keepdims=True) * (1.0 + 2.0 ** -14))
    tiles = jnp.concatenate(tiles, axis=0)
    for m in range(2):
        kn_ref[m] = tiles[:, m:m + 1]


def _pair_flash(q_ref, k_ref, kb_fn, off_fn, decay_fn, finish, i, scratch, *, tq, tk, v_rows, bias_rows,
                bias_sign, split_maps):
    assert tq == 2 * tk
    vt_ref, qt_ref, m_ref, acc_ref, g_ref, bad_ref, kn_ref = scratch[:7]
    s_ref, mt_ref, p_ref, a_ref = ({(slot, h): scratch[7 + 4 * n + 2 * slot + h]
                                    for slot in range(2) for h in range(2)} for n in range(4))
    row = lax.broadcasted_iota(jnp.int32, (LANES, 1), 0)
    lo = row < HEAD_DIM
    qt = (q_ref[...] * jnp.asarray(SCALE, BF16)).T
    zero = jnp.zeros_like(qt)
    for h in range(2):
        qt_ref[h, :LANES, :] = jnp.where(lo, qt, zero) if h == 0 else jnp.where(lo, zero, qt)
        b0, b1 = bias_rows[h]
        qt_ref[h, LANES:, :] = jnp.where((row >= b0) & (row < b1), bias_sign, 0.0).astype(BF16) + zero

    lanes = [(h, slice(c, c + CHUNK)) for c in range(0, tq, CHUNK) for h in range(2)]

    def tile_of(n):
        return jnp.where(n < 2, 2 * i + n, n - 2)

    def raw_scores(j, diag, h, cs):
        j0 = pl.multiple_of(j * tk, tk)
        lhs = jnp.concatenate([k_ref[pl.ds(j0, tk), :], kb_fn(j0)], axis=1)
        t = jnp.dot(lhs, qt_ref[h, :, cs], preferred_element_type=F32)
        if diag is not None:
            kpos = lax.broadcasted_iota(jnp.int32, t.shape, 0) + diag * tk
            qpos = lax.broadcasted_iota(jnp.int32, t.shape, 1) + cs.start
            t = jnp.where(kpos <= qpos, t, NEG_INF)
        return t

    def scores(j, slot, diag, h, cs):
        t = raw_scores(j, diag, h, cs)
        s_ref[slot, h][:, cs] = t
        mt_ref[slot, h][:, cs] = jnp.max(t, axis=0, keepdims=True)

    def values(j, slot, h, cs, first=False):
        r0, r1 = v_rows[h]
        vt = vt_ref[j, r0:r1, :]
        if r1 != LANES + ONES_ROWS:
            vt = jnp.concatenate([vt, vt_ref[j, LANES:, :]], axis=0)
        pv = jnp.dot(vt, p_ref[slot, h][:, cs], preferred_element_type=F32)
        acc_ref[h, :, cs] = pv if first else acc_ref[h, :, cs] * a_ref[slot, h][:, cs] + pv

    def softmax(j, slot, h, cs):
        off = off_fn(j)
        mt = mt_ref[slot, h][:, cs]
        m_old = m_ref[h, :, cs]
        m_new = jnp.maximum(m_old, mt if off is None else mt + off)
        a_ref[1 - slot, h][:, cs] = jnp.exp(m_old - m_new)
        m_ref[h, :, cs] = m_new
        shift = m_new if off is None else m_new - off
        p_ref[1 - slot, h][:, cs] = jnp.exp(s_ref[slot, h][:, cs] - shift).astype(BF16)

    def exact_step(t, slot, diag=None, first=False, last=False):
        for h, cs in lanes:
            if not first:
                values(tile_of(t - 1), slot, h, cs)
            softmax(tile_of(t), slot, h, cs)
            if not last:
                scores(tile_of(t + 1), 1 - slot, diag, h, cs)

    def exact_sweep():
        for h in range(2):
            m_ref[h] = jnp.full((1, tq), NEG_INF, F32)
            acc_ref[h] = jnp.zeros(acc_ref.shape[1:], F32)
        for h, cs in lanes:
            scores(2 * i, 0, 0, h, cs)
        exact_step(0, 0, 1, first=True)

        def two_steps(u, carry):
            exact_step(2 * u + 1, 1)
            exact_step(2 * u + 2, 0)
            return carry

        lax.fori_loop(0, i, two_steps, 0)
        exact_step(2 * i + 1, 1, last=True)
        for h, cs in lanes:
            values(tile_of(2 * i + 1), 0, h, cs)

    def fused(j, slot, diag, h, cs):
        off = off_fn(j)
        t = raw_scores(j, diag, h, cs)
        ref = m_ref[h, :, cs]
        p_ref[slot, h][:, cs] = jnp.exp(t - (ref if off is None else ref - off)).astype(BF16)
        mt = jnp.max(t, axis=0, keepdims=True)
        mt = mt if off is None else mt + off
        a_ref[slot, h][:, cs] = g_ref[h, :, cs]
        m_new = jnp.maximum(ref, mt)
        g_ref[h, :, cs] = jnp.exp(ref - m_new)
        m_ref[h, :, cs] = m_new
        bad_ref[h, :, cs] = jnp.maximum(bad_ref[h, :, cs], mt - ref)

    def first_needed_tile(maps):
        n_tiles = kn_ref.shape[1]
        jt = lax.broadcasted_iota(jnp.int32, (n_tiles, 1), 0)
        needed = jnp.zeros((n_tiles, 1), F32)
        for m in maps:
            qn = jnp.sqrt(_max_sq_norm(q_ref[...], m)) * SCALE
            kn = jnp.sqrt(kn_ref[m])
            own = jnp.max(jnp.where((jt == 2 * i) | (jt == 2 * i + 1), kn, 0.0), axis=0, keepdims=True)
            needed = jnp.maximum(needed, jnp.where(decay_fn(m) <= LOG_SKIP + qn * (kn + own), 1.0, 0.0))
        first = jnp.min(jnp.where((needed > 0.0) & (jt < 2 * i), jt, 2 * i).astype(F32))
        return first.astype(jnp.int32)

    def lagged_step(prev, cur, slot, maps):
        for h, cs in lanes:
            if h in maps:
                values(prev, 1 - slot, h, cs)
                fused(cur, slot, None, h, cs)

    def lagged_sweep():
        d0, d1 = 2 * i, 2 * i + 1
        high = lambda cs: cs.stop > tk
        for h, cs in lanes:
            scores(d1 if high(cs) else d0, 0, 1 if high(cs) else 0, h, cs)
        for h, cs in lanes:
            off = off_fn(d1 if high(cs) else d0)
            mt = mt_ref[0, h][:, cs]
            p_ref[1, h][:, cs] = jnp.exp(s_ref[0, h][:, cs] - mt).astype(BF16)
            m_ref[h, :, cs] = mt if off is None else mt + off
            g_ref[h, :, cs] = jnp.ones((1, CHUNK), F32)
            bad_ref[h, :, cs] = jnp.full((1, CHUNK), NEG_INF, F32)
        for h, cs in lanes:
            values(d1 if high(cs) else d0, 1, h, cs, first=True)
            if high(cs):
                fused(d0, 0, None, h, cs)
            else:
                p_ref[0, h][:, cs] = jnp.zeros((tk, CHUNK), BF16)
                a_ref[0, h][:, cs] = jnp.ones((1, CHUNK), F32)

        groups = ((0,), (1,)) if split_maps else ((0, 1),)
        starts = [first_needed_tile(maps) // 2 for maps in groups]
        pending = {}
        for maps, start in zip(groups, starts):
            def two_steps(u, prev, maps=maps):
                lagged_step(prev, 2 * u, 1, maps)
                lagged_step(2 * u, 2 * u + 1, 0, maps)
                return 2 * u + 1

            prev = lax.fori_loop(start, i, two_steps, d0)
            pending.update({m: prev for m in maps})
        for h, cs in lanes:
            values(pending[h], 0, h, cs)
            if h == 1:
                finish(acc_ref, cs)

    lagged_sweep()
    worst = jnp.max(bad_ref[...])

    @pl.when(jnp.logical_not(worst <= LAG_LIMIT))
    def _():
        exact_sweep()
        for c in range(0, tq, CHUNK):
            finish(acc_ref, slice(c, c + CHUNK))


def _fox_kernel(q_ref, k_ref, v_ref, g_ref, cb_ref, o_ref, ce_ref, *scratch, tq, tk):
    i = pl.program_id(1)
    lane = lax.broadcasted_iota(jnp.int32, (1, LANES), 1)
    pack = 16

    def c_at(rows, last, m):
        terms = cb_ref[rows, :].astype(F32)[pack - 1:pack, :] if last else cb_ref[rows, :].astype(F32)[0:1, :]
        mine = (lane >= N_SPLIT * m) & (lane < N_SPLIT * (m + 1))
        return jnp.sum(jnp.where(mine, terms, 0.0), axis=1, keepdims=True)

    @pl.when(i == 0)
    def _():
        _fill_vt(v_ref, scratch[0], tk)
        _fill_key_norms(k_ref, scratch[6], tk)
        for m in range(2):
            ce_ref[m] = jnp.concatenate([c_at(slice((j + 1) * tk - pack, (j + 1) * tk), True, m)
                                         for j in range(ce_ref.shape[1])], axis=0)

    def decay_fn(m):
        return ce_ref[m] - c_at(pl.ds(pl.multiple_of(i * tq, tq), pack), False, m)

    def finish(acc_ref, cs):
        d = HEAD_DIM
        y = jnp.concatenate([acc_ref[0, :d, cs] / acc_ref[0, d:d + 1, cs],
                             acc_ref[1, :d, cs] / acc_ref[1, d:d + 1, cs]], axis=0).T
        o_ref[cs, :] = (y * _silu(g_ref[cs, :].astype(F32))).astype(o_ref.dtype)

    _pair_flash(q_ref, k_ref, lambda j0: cb_ref[pl.ds(j0, tk), :], lambda j: None, decay_fn, finish, i, scratch,
                tq=tq, tk=tk, v_rows=((0, HEAD_DIM), (HEAD_DIM, LANES)),
                bias_rows=((0, N_SPLIT), (N_SPLIT, 2 * N_SPLIT)), bias_sign=-1.0, split_maps=True)


def _fox_attention(z, cb, *, tq, tk):
    s = z.shape[0]
    nb = C_WIDTH // LANES
    return pl.pallas_call(
        functools.partial(_fox_kernel, tq=tq, tk=tk),
        grid=(nb, s // tq),
        in_specs=[
            pl.BlockSpec((tq, LANES), lambda p, i: (i, p)),
            pl.BlockSpec((s, LANES), lambda p, i: (0, nb + p)),
            pl.BlockSpec((s, LANES), lambda p, i: (0, 2 * nb + p)),
            pl.BlockSpec((tq, LANES), lambda p, i: (i, 3 * nb + p)),
            pl.BlockSpec((s, LANES), lambda p, i: (0, p)),
        ],
        out_specs=pl.BlockSpec((tq, LANES), lambda p, i: (i, p)),
        out_shape=jax.ShapeDtypeStruct((s, C_WIDTH), BF16),
        scratch_shapes=[pltpu.VMEM((2, s // tk, 1), F32)] + _flash_scratch(s, tq, tk, HEAD_DIM),
        compiler_params=_cparams(("parallel", "arbitrary")),
    )(z, z, z, z, cb)


def _diff_kernel(q_ref, k_ref, v_ref, g_ref, slope_ref, lam_ref, sg_ref, o_ref, kb_ref, *scratch,
                 tq, tk, lam_init):
    i = pl.program_id(1)
    slope = slope_ref[...]

    @pl.when(i == 0)
    def _():
        _fill_vt(v_ref, scratch[0], tk)
        _fill_key_norms(k_ref, scratch[6], tk)
        r = lax.broadcasted_iota(jnp.int32, (tk, LANES), 0).astype(F32)
        lane = lax.broadcasted_iota(jnp.int32, (tk, LANES), 1)
        terms = _split_bf16(slope * r)
        kb = jnp.zeros((tk, LANES), F32)
        for n, t in enumerate(terms):
            kb = jnp.where(lane == n, t.astype(F32), kb)
        kb_ref[...] = kb.astype(BF16)

    def off_fn(j):
        return slope * (j * tk - i * tq).astype(F32)

    def decay_fn(m):
        ends = lax.broadcasted_iota(jnp.int32, (scratch[6].shape[1], 1), 0) * tk + (tk - 1)
        return slope * (i * tq - ends).astype(F32)

    lp = lam_ref[...]
    lam = (jnp.exp(jnp.sum(lp[0:1] * lp[1:2], axis=-1, keepdims=True))
           - jnp.exp(jnp.sum(lp[2:3] * lp[3:4], axis=-1, keepdims=True)) + lam_init)

    def finish(acc_ref, cs):
        d = LANES
        o = (acc_ref[0, :d, cs] / acc_ref[0, d:d + 1, cs]
             - lam * (acc_ref[1, :d, cs] / acc_ref[1, d:d + 1, cs])).T
        o = o * lax.rsqrt(jnp.mean(o * o, axis=-1, keepdims=True) + RMS_EPS) * sg_ref[...]
        o = o * (1.0 - lam_init)
        o_ref[cs, :] = (o * _silu(g_ref[cs, :].astype(F32))).astype(o_ref.dtype)

    _pair_flash(q_ref, k_ref, lambda j0: kb_ref[...], off_fn, decay_fn, finish, i, scratch,
                tq=tq, tk=tk, v_rows=((0, LANES + ONES_ROWS), (0, LANES + ONES_ROWS)),
                bias_rows=((0, N_SPLIT), (0, N_SPLIT)), bias_sign=1.0, split_maps=False)


def _diff_attention(z, slopes, lam_params, subln_g, lam_init, *, tq, tk):
    s = z.shape[0]
    qb, kb, vb, gb = 18, 26, 34, 42
    return pl.pallas_call(
        functools.partial(_diff_kernel, tq=tq, tk=tk, lam_init=lam_init),
        grid=(B_HEADS, s // tq),
        in_specs=[
            pl.BlockSpec((tq, LANES), lambda h, i: (i, qb + h)),
            pl.BlockSpec((s, LANES), lambda h, i: (0, kb + h)),
            pl.BlockSpec((s, LANES), lambda h, i: (0, vb + h)),
            pl.BlockSpec((tq, LANES), lambda h, i: (i, gb + h)),
            pl.BlockSpec((None, 1, 1), lambda h, i: (h, 0, 0)),
            pl.BlockSpec((4, HEAD_DIM), lambda h, i: (0, 0)),
            pl.BlockSpec((1, LANES), lambda h, i: (0, 0)),
        ],
        out_specs=pl.BlockSpec((tq, LANES), lambda h, i: (i, h)),
        out_shape=jax.ShapeDtypeStruct((s, B_WIDTH), BF16),
        scratch_shapes=[pltpu.VMEM((tk, LANES), BF16)] + _flash_scratch(s, tq, tk, LANES),
        compiler_params=_cparams(("parallel", "arbitrary")),
    )(z, z, z, z, slopes.reshape(B_HEADS, 1, 1), lam_params, subln_g.reshape(1, LANES))


def _swa_kernel(slope_ref, sink_ref, q_ref, kp_ref, kc_ref, vp_ref, vc_ref, *rest):
    *g_refs, o_ref = rest
    i = pl.program_id(0)
    blk = WINDOW
    lane = lax.broadcasted_iota(jnp.int32, (1, LANES), 1)
    lo = lane < HEAD_DIM

    def swap_halves(a):
        return pltpu.roll(a.astype(F32), HEAD_DIM, 1).astype(BF16)

    k = jnp.concatenate([kp_ref[...], kc_ref[...]], axis=0)
    v = jnp.concatenate([vp_ref[...], vc_ref[...]], axis=0)
    k_sw, v_sw = swap_halves(k), swap_halves(v)

    r = lax.broadcasted_iota(jnp.int32, (blk, 2 * blk), 0)
    c = lax.broadcasted_iota(jnp.int32, (blk, 2 * blk), 1)
    dist = r - c + blk
    valid = (dist >= 0) & (dist < WINDOW) & ((c >= blk) | (i > 0))
    neg_dist = jnp.where(valid, -dist.astype(F32), NEG_INF)

    for pair in range(A_HEADS // 2):
        qp = q_ref[:, pair * LANES:(pair + 1) * LANES] * jnp.asarray(SCALE, BF16)
        zero = jnp.zeros_like(qp)
        outs = []
        for a in range(2):
            h = 2 * pair + a
            kv = h // A_GROUP
            qh = jnp.where(lo, qp, zero) if a == 0 else jnp.where(lo, zero, qp)
            kk, vv = (k, v) if a == kv else (k_sw, v_sw)
            s = lax.dot_general(qh, kk, (((1,), (1,)), ((), ())), preferred_element_type=F32)
            s = s + slope_ref[h] * neg_dist
            sink = sink_ref[h]
            m = jnp.maximum(jnp.max(s, axis=-1, keepdims=True), sink)
            e = jnp.exp(s - m)
            denom = jnp.sum(e, axis=-1, keepdims=True) + jnp.exp(sink - m)
            outs.append(jnp.dot(e.astype(BF16), vv, preferred_element_type=F32) / denom)
        y = jnp.where(lo, outs[0], outs[1])
        g_ref = g_refs[pair // 2]
        g = g_ref[:, (pair % 2) * LANES:(pair % 2 + 1) * LANES].astype(F32)
        o_ref[:, pair * LANES:(pair + 1) * LANES] = (y * _silu(g)).astype(o_ref.dtype)


def _swa_attention(z, slopes, sinks):
    s = z.shape[0]
    blk = WINDOW
    nq = A_WIDTH // LANES
    kb, vb = nq, nq + 1
    gw = 2 * LANES
    g_first = (nq + 2) * LANES // gw
    assert g_first * gw == (nq + 2) * LANES
    smem = pl.BlockSpec(memory_space=pltpu.SMEM)
    prev = lambda i: (jnp.maximum(i - 1, 0), kb)
    prev_v = lambda i: (jnp.maximum(i - 1, 0), vb)
    return pl.pallas_call(
        _swa_kernel,
        grid=(s // blk,),
        in_specs=[
            smem, smem,
            pl.BlockSpec((blk, A_WIDTH), lambda i: (i, 0)),
            pl.BlockSpec((blk, LANES), prev),
            pl.BlockSpec((blk, LANES), lambda i: (i, kb)),
            pl.BlockSpec((blk, LANES), prev_v),
            pl.BlockSpec((blk, LANES), lambda i: (i, vb)),
            *[pl.BlockSpec((blk, gw), lambda i, c=g_first + n: (i, c)) for n in range(A_WIDTH // gw)],
        ],
        out_specs=pl.BlockSpec((blk, A_WIDTH), lambda i: (i, 0)),
        out_shape=jax.ShapeDtypeStruct((s, A_WIDTH), BF16),
        compiler_params=_cparams(("parallel",)),
    )(slopes, sinks, *([z] * (5 + A_WIDTH // gw)))


def _out_kernel(*refs, n_y, final):
    x_ref = refs[0]
    y_refs = refs[1:1 + n_y]
    w_refs = refs[1 + n_y:1 + 2 * n_y]
    p_ref, wp_ref, gn_ref, wg_ref = refs[1 + 2 * n_y:5 + 2 * n_y]
    if final:
        fg_ref, o_ref = refs[5 + 2 * n_y:]
    else:
        (o_ref,) = refs[5 + 2 * n_y:]

    x1 = x_ref[...]
    for y_ref, w_ref in zip(y_refs, w_refs):
        x1 = x1 + jnp.dot(y_ref[...], w_ref[...], preferred_element_type=F32)
    hn = x1 * lax.rsqrt(jnp.mean(x1 * x1, axis=-1, keepdims=True) + RMS_EPS) * gn_ref[...]
    gate = jax.nn.sigmoid(jnp.dot(hn.astype(BF16), wg_ref[...], preferred_element_type=F32))
    pp = jnp.dot(p_ref[...].astype(BF16), wp_ref[...], preferred_element_type=F32)
    x2 = x1 + gate * pp
    if final:
        x2 = x2 * lax.rsqrt(jnp.mean(x2 * x2, axis=-1, keepdims=True) + RMS_EPS) * fg_ref[...]
    o_ref[...] = x2


def _out_block(x, ys, w, p, wp, gn, wg, final_g=None, *, tm):
    s, d = x.shape
    n_y = len(ys)
    final = final_g is not None
    const = lambda i: (0, 0)
    single = pl.Buffered(1)
    in_specs = [pl.BlockSpec((tm, d), lambda i: (i, 0))]
    in_specs += [pl.BlockSpec((tm, y.shape[1]), lambda i: (i, 0)) for y in ys]
    in_specs += [pl.BlockSpec((y.shape[1], d), lambda i, r=r: (r, 0), pipeline_mode=single)
                 for r, y in enumerate(ys)]
    assert all(y.shape[1] == ys[0].shape[1] for y in ys) and len(ys) * ys[0].shape[1] == w.shape[0]
    ws = [w] * n_y
    in_specs += [
        pl.BlockSpec((tm, p.shape[1]), lambda i: (i, 0)),
        pl.BlockSpec(wp.shape, const, pipeline_mode=single),
        pl.BlockSpec((1, d), const),
        pl.BlockSpec(wg.shape, const, pipeline_mode=single),
    ]
    args = [x, *ys, *ws, p, wp, gn.reshape(1, d), wg]
    if final:
        in_specs.append(pl.BlockSpec((1, d), const))
        args.append(final_g.reshape(1, d))
    return pl.pallas_call(
        functools.partial(_out_kernel, n_y=n_y, final=final),
        grid=(s // tm,),
        in_specs=in_specs,
        out_specs=pl.BlockSpec((tm, d), lambda i: (i, 0)),
        out_shape=jax.ShapeDtypeStruct((s, d), F32),
        compiler_params=_cparams(("parallel",)),
    )(*args)


def _alibi_slopes(n):
    return jnp.asarray([2.0 ** (-8.0 * (h + 1) / n) for h in range(n)], dtype=F32)


def kernel(x, p, norm_g, w_in_ab, w_out_ab, attn_sinks, diff_lambda, diff_subln_g,
           w_in_c, w_out_c, forget_bias, ple_proj, ple_gate, ple_norm_g, final_norm_g):
    b, s, d = x.shape
    assert (b, s, d) == (1, SEQ, D_MODEL)
    xs = x.reshape(s, d)
    for i in range(DEPTH):
        j = i // 2
        last = i == DEPTH - 1
        if i % 2 == 0:
            lam_init = 0.8 - 0.6 * math.exp(-0.3 * i)
            tn = 1280
            z = _norm_matmul(xs, norm_g[i], [(w_in_ab[j].astype(BF16), 0, AB_IN // tn)], tm=1024, tn=tn)
            ya = _swa_attention(z, _alibi_slopes(A_HEADS), attn_sinks[j].astype(F32))
            yb = _diff_attention(z, _alibi_slopes(B_HEADS), diff_lambda[j].astype(F32),
                                 diff_subln_g[j].astype(F32), lam_init, tq=1024, tk=512)
            ys, w_out = [ya, yb], w_out_ab[j].astype(BF16)
        else:
            tn = 1024
            w = w_in_c[j].astype(BF16)
            w_g = w_in_c[j][:, 3 * C_WIDTH + C_HEADS:].astype(BF16)
            z, f = _norm_matmul(xs, norm_g[i], [(w, 0, 3 * C_WIDTH // tn), (w_g, 0, C_WIDTH // tn)],
                                (w, 3 * C_WIDTH // LANES), tm=1024, tn=tn)
            f_bias = jnp.pad(forget_bias[j].astype(F32), (0, LANES - C_HEADS)).reshape(1, LANES)
            cb = _logsig_cumsum(f, f_bias, C_HEADS // 2)
            ys, w_out = [_fox_attention(z, cb, tq=1024, tk=512)], w_out_c[j].astype(BF16)
        xs = _out_block(xs, ys, w_out, p[i].reshape(s, D_PLE), ple_proj[i].astype(BF16),
                        ple_norm_g[i], ple_gate[i].astype(BF16),
                        final_norm_g if last else None, tm=256)
    return xs.reshape(b, s, d)
```

```python
import functools
import itertools
import math

import jax
import jax.numpy as jnp
from jax import lax
from jax.experimental import pallas as pl
from jax.experimental.pallas import tpu as pltpu

F32 = jnp.float32
BF16 = jnp.bfloat16

D_MODEL = 2048
SEQ = 8192
DEPTH = 2
HEAD_DIM = 64
D_PLE = 256
WINDOW = 128
RMS_EPS = 1e-6
A_HEADS = 16
A_KV_HEADS = 2
A_GROUP = A_HEADS // A_KV_HEADS
A_WIDTH = A_HEADS * HEAD_DIM
B_HEADS = 8
B_WIDTH = B_HEADS * 2 * HEAD_DIM
C_HEADS = 32
C_WIDTH = C_HEADS * HEAD_DIM
AB_IN = 6400
SCALE = HEAD_DIM ** -0.5

LANES = 128
NEG_INF = float("-inf")
VMEM_LIMIT = 48 * 1024 * 1024


def _cparams(sem):
    return pltpu.CompilerParams(dimension_semantics=sem, vmem_limit_bytes=VMEM_LIMIT)


def _silu(g):
    return g * (1.0 / (1.0 + jnp.exp(-g)))


NORM_ROWS = 256


def _norm_matmul_kernel(x_ref, g_ref, *rest, seg_ends, with_extra):
    w_refs, rest = rest[:len(seg_ends)], rest[len(seg_ends):]
    if with_extra:
        we_ref, o_ref, oe_ref, h_ref = rest
    else:
        o_ref, h_ref = rest
    j = pl.program_id(1)

    @pl.when(j == 0)
    def _():
        def norm_rows(r, carry):
            rows = pl.ds(pl.multiple_of(r * NORM_ROWS, NORM_ROWS), NORM_ROWS)
            x = x_ref[rows, :]
            ms = jnp.mean(x * x, axis=-1, keepdims=True)
            h_ref[rows, :] = (x * lax.rsqrt(ms + RMS_EPS) * g_ref[...]).astype(BF16)
            return carry
        lax.fori_loop(0, x_ref.shape[0] // NORM_ROWS, norm_rows, 0)
        if with_extra:
            oe_ref[...] = jnp.dot(h_ref[...], we_ref[...], preferred_element_type=F32)

    def project(w_ref):
        o_ref[...] = jnp.dot(h_ref[...], w_ref[...], preferred_element_type=F32).astype(o_ref.dtype)

    if len(w_refs) == 1:
        project(w_refs[0])
    else:
        for lo, hi, w_ref in zip((0,) + seg_ends[:-1], seg_ends, w_refs):
            pl.when((j >= lo) & (j < hi))(functools.partial(project, w_ref))


def _norm_matmul(x, g, segments, extra=None, *, tm, tn):
    s, d = x.shape
    seg_ends = tuple(itertools.accumulate(count for _, _, count in segments))
    with_extra = extra is not None
    in_specs = [
        pl.BlockSpec((tm, d), lambda i, j: (i, 0)),
        pl.BlockSpec((1, d), lambda i, j: (0, 0)),
    ]
    args = [x, g.reshape(1, d)]
    for (w, first, count), end in zip(segments, seg_ends):
        in_specs.append(pl.BlockSpec(
            (d, tn), lambda i, j, first=first, count=count, lo=end - count:
            (0, first + jnp.clip(j - lo, 0, count - 1))))
        args.append(w)
    n = seg_ends[-1] * tn
    out_shape = [jax.ShapeDtypeStruct((s, n), BF16)]
    out_specs = [pl.BlockSpec((tm, tn), lambda i, j: (i, j))]
    if with_extra:
        we, tile = extra
        in_specs.append(pl.BlockSpec((d, LANES), lambda i, j: (0, tile)))
        out_shape.append(jax.ShapeDtypeStruct((s, LANES), F32))
        out_specs.append(pl.BlockSpec((tm, LANES), lambda i, j: (i, 0)))
        args.append(we)
    res = pl.pallas_call(
        functools.partial(_norm_matmul_kernel, seg_ends=seg_ends, with_extra=with_extra),
        grid=(s // tm, n // tn),
        in_specs=in_specs,
        out_specs=out_specs,
        out_shape=out_shape,
        scratch_shapes=[pltpu.VMEM((tm, d), BF16)],
        compiler_params=_cparams(("parallel", "arbitrary")),
    )(*args)
    return res if with_extra else res[0]


N_SPLIT = 3


def _split_bf16(x):
    parts = []
    for _ in range(N_SPLIT):
        t = x.astype(BF16)
        parts.append(t)
        x = x - t.astype(F32)
    return parts


def _logsig_cumsum_kernel(f_ref, b_ref, o_ref, carry_ref, sel_ref, *, blk):
    n = f_ref.shape[1]

    @pl.when(pl.program_id(0) == 0)
    def _():
        carry_ref[...] = jnp.zeros_like(carry_ref)
        r = lax.broadcasted_iota(jnp.int32, sel_ref.shape, 0)
        l = lax.broadcasted_iota(jnp.int32, sel_ref.shape, 1)
        part, head = r // n, r % n
        lane = jnp.where(l // LANES == head // 2, l % LANES, -1)
        sel_ref[...] = jnp.where(lane == N_SPLIT * (head % 2) + part, 1.0, 0.0).astype(BF16)

    z = f_ref[...] + b_ref[...]
    ls = jnp.minimum(z, 0.0) - jnp.log1p(jnp.exp(-jnp.abs(z)))
    row = lax.broadcasted_iota(jnp.int32, (blk, blk), 0)
    col = lax.broadcasted_iota(jnp.int32, (blk, blk), 1)
    lower = (col <= row).astype(F32)
    c = jnp.dot(lower, ls, preferred_element_type=F32,
                precision=lax.Precision.HIGHEST) + carry_ref[...]
    carry_ref[...] = c[blk - 1:blk, :]
    terms = jnp.concatenate(_split_bf16(c), axis=1)
    o_ref[...] = jnp.dot(terms, sel_ref[...], preferred_element_type=F32).astype(BF16)


def _logsig_cumsum(f, bias_row, n_pairs, *, blk=256):
    s, n = f.shape
    return pl.pallas_call(
        functools.partial(_logsig_cumsum_kernel, blk=blk),
        grid=(s // blk,),
        in_specs=[pl.BlockSpec((blk, n), lambda i: (i, 0)),
                  pl.BlockSpec((1, n), lambda i: (0, 0))],
        out_specs=pl.BlockSpec((blk, n_pairs * LANES), lambda i: (i, 0)),
        out_shape=jax.ShapeDtypeStruct((s, n_pairs * LANES), BF16),
        scratch_shapes=[pltpu.VMEM((1, n), F32), pltpu.VMEM((N_SPLIT * n, n_pairs * LANES), BF16)],
        compiler_params=_cparams(("arbitrary",)),
    )(f, bias_row)


ONES_ROWS = 16
CHUNK = 256
MIN_DENOMINATOR = 1e-17
LOG_SKIP = 94.0


def _flash_scratch(s, tq, tk, rows):
    return [
        pltpu.VMEM((s // tk, LANES + ONES_ROWS, tk), BF16),
        pltpu.VMEM((2, 2 * LANES, tq), BF16),
        pltpu.VMEM((2, 1, tq), F32),
        pltpu.VMEM((2, rows + ONES_ROWS, tq), F32),
        pltpu.VMEM((2, s // tk, 1), F32),
        *[pltpu.VMEM((tk, tq), F32) for _ in range(4)],
        *[pltpu.VMEM((1, tq), F32) for _ in range(4)],
        *[pltpu.VMEM((tk, tq), BF16) for _ in range(4)],
        *[pltpu.VMEM((1, tq), F32) for _ in range(4)],
    ]


def _fill_vt(v_ref, vt_ref, tk):
    def body(c, carry):
        vt_ref[c, :LANES, :] = v_ref[pl.ds(pl.multiple_of(c * tk, tk), tk), :].T
        vt_ref[c, LANES:, :] = jnp.ones((ONES_ROWS, tk), BF16)
        return carry
    lax.fori_loop(0, vt_ref.shape[0], body, 0)


def _map_lanes(m):
    lane = lax.broadcasted_iota(jnp.int32, (1, LANES), 1)
    return (lane < HEAD_DIM) if m == 0 else (lane >= HEAD_DIM)


def _max_sq_norm(x, m):
    x = x.astype(F32)
    rows = jnp.sum(jnp.where(_map_lanes(m), x * x, 0.0), axis=1, keepdims=True)
    return jnp.max(rows, axis=0, keepdims=True)


def _fill_key_norms(k_ref, kn_ref, tk):
    r = lax.broadcasted_iota(jnp.int32, (LANES, LANES), 0)
    c = lax.broadcasted_iota(jnp.int32, (LANES, LANES), 1)
    sel = jnp.where(c == jnp.where(r < HEAD_DIM, 0, 1), 1.0, 0.0).astype(BF16)
    tiles = []
    for j in range(kn_ref.shape[1]):
        kt = k_ref[j * tk:(j + 1) * tk, :].astype(F32)
        sq = kt * kt
        hi = sq.astype(BF16)
        lo = (sq - hi.astype(F32)).astype(BF16)
        sums = (jnp.dot(hi, sel, preferred_element_type=F32) + jnp.dot(lo, sel, preferred_element_type=F32))
        tiles.append(jnp.max(sums, axis=0, keepdims=True) * (1.0 + 2.0 ** -14))
    tiles = jnp.concatenate(tiles, axis=0)
    for m in range(2):
        kn_ref[m] = tiles[:, m:m + 1]


def _pair_flash(q_ref, k_ref, kb_fn, off_fn, decay_fn, bias_row_fn, finish, i, scratch, *, tq, tk, v_rows,
                bias_rows, bias_sign, split_maps):
    assert tq == 2 * tk
    vt_ref, qt_ref, m_ref, acc_ref, kn_ref = scratch[:5]
    s_ref, mt_ref, p_ref, a_ref = ({(slot, h): scratch[5 + 4 * n + 2 * slot + h]
                                    for slot in range(2) for h in range(2)} for n in range(4))
    row = lax.broadcasted_iota(jnp.int32, (LANES, 1), 0)
    lo = row < HEAD_DIM
    qt = (q_ref[...] * jnp.asarray(SCALE, BF16)).T
    zero = jnp.zeros_like(qt)
    for h in range(2):
        qt_ref[h, :LANES, :] = jnp.where(lo, qt, zero) if h == 0 else jnp.where(lo, zero, qt)
        b0, b1 = bias_rows[h]
        qt_ref[h, LANES:, :] = jnp.where((row >= b0) & (row < b1), bias_sign, 0.0).astype(BF16) + zero

    lanes = [(h, slice(c, c + CHUNK)) for c in range(0, tq, CHUNK) for h in range(2)]

    def tile_of(n):
        return jnp.where(n < 2, 2 * i + n, n - 2)

    def raw_scores(j, diag, h, cs):
        j0 = pl.multiple_of(j * tk, tk)
        lhs = jnp.concatenate([k_ref[pl.ds(j0, tk), :], kb_fn(j0)], axis=1)
        t = jnp.dot(lhs, qt_ref[h, :, cs], preferred_element_type=F32)
        if diag is not None:
            kpos = lax.broadcasted_iota(jnp.int32, t.shape, 0) + diag * tk
            qpos = lax.broadcasted_iota(jnp.int32, t.shape, 1) + cs.start
            t = jnp.where(kpos <= qpos, t, NEG_INF)
        return t

    def scores(j, slot, diag, h, cs):
        t = raw_scores(j, diag, h, cs)
        s_ref[slot, h][:, cs] = t
        mt_ref[slot, h][:, cs] = jnp.max(t, axis=0, keepdims=True)

    def values(j, slot, h, cs, first=False, rescale=True):
        r0, r1 = v_rows[h]
        vt = vt_ref[j, r0:r1, :]
        if r1 != LANES + ONES_ROWS:
            vt = jnp.concatenate([vt, vt_ref[j, LANES:, :]], axis=0)
        pv = jnp.dot(vt, p_ref[slot, h][:, cs], preferred_element_type=F32)
        if first:
            acc_ref[h, :, cs] = pv
        elif rescale:
            acc_ref[h, :, cs] = acc_ref[h, :, cs] * a_ref[slot, h][:, cs] + pv
        else:
            acc_ref[h, :, cs] = acc_ref[h, :, cs] + pv

    def softmax(j, slot, h, cs):
        off = off_fn(j)
        mt = mt_ref[slot, h][:, cs]
        m_old = m_ref[h, :, cs]
        m_new = jnp.maximum(m_old, mt if off is None else mt + off)
        a_ref[1 - slot, h][:, cs] = jnp.exp(m_old - m_new)
        m_ref[h, :, cs] = m_new
        shift = m_new if off is None else m_new - off
        p_ref[1 - slot, h][:, cs] = jnp.exp(s_ref[slot, h][:, cs] - shift).astype(BF16)

    def exact_step(t, slot, diag=None, first=False, last=False):
        for h, cs in lanes:
            if not first:
                values(tile_of(t - 1), slot, h, cs)
            softmax(tile_of(t), slot, h, cs)
            if not last:
                scores(tile_of(t + 1), 1 - slot, diag, h, cs)

    def exact_sweep():
        for h in range(2):
            m_ref[h] = jnp.full((1, tq), NEG_INF, F32)
            acc_ref[h] = jnp.zeros(acc_ref.shape[1:], F32)
        for h, cs in lanes:
            scores(2 * i, 0, 0, h, cs)
        exact_step(0, 0, 1, first=True)

        def two_steps(u, carry):
            exact_step(2 * u + 1, 1)
            exact_step(2 * u + 2, 0)
            return carry

        lax.fori_loop(0, i, two_steps, 0)
        exact_step(2 * i + 1, 1, last=True)
        for h, cs in lanes:
            values(tile_of(2 * i + 1), 0, h, cs)

    def fused(j, slot, diag, h, cs):
        off = off_fn(j)
        ub = m_ref[h, :, cs]
        t = raw_scores(j, diag, h, cs)
        p_ref[slot, h][:, cs] = jnp.exp(t - (ub if off is None else ub - off)).astype(BF16)

    def first_needed_tile(maps):
        n_tiles = kn_ref.shape[1]
        jt = lax.broadcasted_iota(jnp.int32, (n_tiles, 1), 0)
        needed = jnp.zeros((n_tiles, 1), F32)
        for m in maps:
            qn = jnp.sqrt(_max_sq_norm(q_ref[...], m)) * SCALE
            kn = jnp.sqrt(kn_ref[m])
            own = jnp.max(jnp.where((jt == 2 * i) | (jt == 2 * i + 1), kn, 0.0), axis=0, keepdims=True)
            needed = jnp.maximum(needed, jnp.where(decay_fn(m) <= LOG_SKIP + qn * (kn + own), 1.0, 0.0))
        first = jnp.min(jnp.where((needed > 0.0) & (jt < 2 * i), jt, 2 * i).astype(F32))
        return first.astype(jnp.int32)

    def bounded_step(prev, cur, slot, maps):
        for h, cs in lanes:
            if h in maps:
                values(prev, 1 - slot, h, cs, rescale=False)
                fused(cur, slot, None, h, cs)

    def bounded_sweep():
        d0, d1 = 2 * i, 2 * i + 1
        high = lambda cs: cs.stop > tk
        qsq = jnp.square(qt.astype(F32))
        for m in range(2):
            qn = jnp.sqrt(jnp.sum(qsq[m * HEAD_DIM:(m + 1) * HEAD_DIM], axis=0, keepdims=True))
            kmax = jnp.sqrt(jnp.max(kn_ref[m], axis=0, keepdims=True))
            m_ref[m] = bias_row_fn(m) + qn * kmax * (1.0 + 2.0 ** -14)
        for h, cs in lanes:
            fused(d1 if high(cs) else d0, 1, 1 if high(cs) else 0, h, cs)
        for h, cs in lanes:
            values(d1 if high(cs) else d0, 1, h, cs, first=True)
            if high(cs):
                fused(d0, 0, None, h, cs)
            else:
                p_ref[0, h][:, cs] = jnp.zeros((tk, CHUNK), BF16)

        groups = ((0,), (1,)) if split_maps else ((0, 1),)
        starts = [first_needed_tile(maps) // 2 for maps in groups]
        pending = {}
        for maps, start in zip(groups, starts):
            def two_steps(u, prev, maps=maps):
                bounded_step(prev, 2 * u, 1, maps)
                bounded_step(2 * u, 2 * u + 1, 0, maps)
                return 2 * u + 1

            prev = lax.fori_loop(start, i, two_steps, d0)
            pending.update({m: prev for m in maps})
        for h, cs in lanes:
            values(pending[h], 0, h, cs, rescale=False)
            if h == 1:
                finish(acc_ref, cs)

    bounded_sweep()
    den = acc_ref.shape[1] - ONES_ROWS
    smallest = jnp.minimum(jnp.min(acc_ref[0, den:den + 1, :]), jnp.min(acc_ref[1, den:den + 1, :]))

    @pl.when(jnp.logical_not(smallest >= MIN_DENOMINATOR))
    def _():
        exact_sweep()
        for c in range(0, tq, CHUNK):
            finish(acc_ref, slice(c, c + CHUNK))


def _fox_kernel(q_ref, k_ref, v_ref, g_ref, cb_ref, o_ref, ce_ref, *scratch, tq, tk):
    i = pl.program_id(1)
    lane = lax.broadcasted_iota(jnp.int32, (1, LANES), 1)
    pack = 16

    def c_at(rows, last, m):
        terms = cb_ref[rows, :].astype(F32)[pack - 1:pack, :] if last else cb_ref[rows, :].astype(F32)[0:1, :]
        mine = (lane >= N_SPLIT * m) & (lane < N_SPLIT * (m + 1))
        return jnp.sum(jnp.where(mine, terms, 0.0), axis=1, keepdims=True)

    @pl.when(i == 0)
    def _():
        _fill_vt(v_ref, scratch[0], tk)
        _fill_key_norms(k_ref, scratch[4], tk)
        for m in range(2):
            ce_ref[m] = jnp.concatenate([c_at(slice((j + 1) * tk - pack, (j + 1) * tk), True, m)
                                         for j in range(ce_ref.shape[1])], axis=0)

    def decay_fn(m):
        return ce_ref[m] - c_at(pl.ds(pl.multiple_of(i * tq, tq), pack), False, m)

    def bias_row_fn(m):
        r = lax.broadcasted_iota(jnp.int32, (8, LANES), 0)
        sel = jnp.where((r == m) & (lane >= N_SPLIT * m) & (lane < N_SPLIT * (m + 1)), 1.0, 0.0).astype(BF16)
        c_rows = lax.dot_general(sel, cb_ref[pl.ds(pl.multiple_of(i * tq, tq), tq), :],
                                 (((1,), (1,)), ((), ())), preferred_element_type=F32)
        return -c_rows[m:m + 1, :]

    def finish(acc_ref, cs):
        d = HEAD_DIM
        y = jnp.concatenate([acc_ref[0, :d, cs] / acc_ref[0, d:d + 1, cs],
                             acc_ref[1, :d, cs] / acc_ref[1, d:d + 1, cs]], axis=0).T
        o_ref[cs, :] = (y * _silu(g_ref[cs, :].astype(F32))).astype(o_ref.dtype)

    _pair_flash(q_ref, k_ref, lambda j0: cb_ref[pl.ds(j0, tk), :], lambda j: None, decay_fn, bias_row_fn, finish,
                i, scratch,
                tq=tq, tk=tk, v_rows=((0, HEAD_DIM), (HEAD_DIM, LANES)),
                bias_rows=((0, N_SPLIT), (N_SPLIT, 2 * N_SPLIT)), bias_sign=-1.0, split_maps=True)


def _fox_attention(z, cb, *, tq, tk):
    s = z.shape[0]
    nb = C_WIDTH // LANES
    return pl.pallas_call(
        functools.partial(_fox_kernel, tq=tq, tk=tk),
        grid=(nb, s // tq),
        in_specs=[
            pl.BlockSpec((tq, LANES), lambda p, i: (i, p)),
            pl.BlockSpec((s, LANES), lambda p, i: (0, nb + p)),
            pl.BlockSpec((s, LANES), lambda p, i: (0, 2 * nb + p)),
            pl.BlockSpec((tq, LANES), lambda p, i: (i, 3 * nb + p)),
            pl.BlockSpec((s, LANES), lambda p, i: (0, p)),
        ],
        out_specs=pl.BlockSpec((tq, LANES), lambda p, i: (i, p)),
        out_shape=jax.ShapeDtypeStruct((s, C_WIDTH), BF16),
        scratch_shapes=[pltpu.VMEM((2, s // tk, 1), F32)] + _flash_scratch(s, tq, tk, HEAD_DIM),
        compiler_params=_cparams(("parallel", "arbitrary")),
    )(z, z, z, z, cb)


def _diff_kernel(q_ref, k_ref, v_ref, g_ref, slope_ref, lam_ref, sg_ref, o_ref, kb_ref, *scratch,
                 tq, tk, lam_init):
    i = pl.program_id(1)
    slope = slope_ref[...]

    @pl.when(i == 0)
    def _():
        _fill_vt(v_ref, scratch[0], tk)
        _fill_key_norms(k_ref, scratch[4], tk)
        r = lax.broadcasted_iota(jnp.int32, (tk, LANES), 0).astype(F32)
        lane = lax.broadcasted_iota(jnp.int32, (tk, LANES), 1)
        terms = _split_bf16(slope * r)
        kb = jnp.zeros((tk, LANES), F32)
        for n, t in enumerate(terms):
            kb = jnp.where(lane == n, t.astype(F32), kb)
        kb_ref[...] = kb.astype(BF16)

    def off_fn(j):
        return slope * (j * tk - i * tq).astype(F32)

    def decay_fn(m):
        ends = lax.broadcasted_iota(jnp.int32, (scratch[4].shape[1], 1), 0) * tk + (tk - 1)
        return slope * (i * tq - ends).astype(F32)

    def bias_row_fn(m):
        return slope * lax.broadcasted_iota(jnp.int32, (1, tq), 1).astype(F32)

    lp = lam_ref[...]
    lam = (jnp.exp(jnp.sum(lp[0:1] * lp[1:2], axis=-1, keepdims=True))
           - jnp.exp(jnp.sum(lp[2:3] * lp[3:4], axis=-1, keepdims=True)) + lam_init)

    def finish(acc_ref, cs):
        d = LANES
        o = (acc_ref[0, :d, cs] / acc_ref[0, d:d + 1, cs]
             - lam * (acc_ref[1, :d, cs] / acc_ref[1, d:d + 1, cs])).T
        o = o * lax.rsqrt(jnp.mean(o * o, axis=-1, keepdims=True) + RMS_EPS) * sg_ref[...]
        o = o * (1.0 - lam_init)
        o_ref[cs, :] = (o * _silu(g_ref[cs, :].astype(F32))).astype(o_ref.dtype)

    _pair_flash(q_ref, k_ref, lambda j0: kb_ref[...], off_fn, decay_fn, bias_row_fn, finish, i, scratch,
                tq=tq, tk=tk, v_rows=((0, LANES + ONES_ROWS), (0, LANES + ONES_ROWS)),
                bias_rows=((0, N_SPLIT), (0, N_SPLIT)), bias_sign=1.0, split_maps=False)


def _diff_attention(z, slopes, lam_params, subln_g, lam_init, *, tq, tk):
    s = z.shape[0]
    qb, kb, vb, gb = 18, 26, 34, 42
    return pl.pallas_call(
        functools.partial(_diff_kernel, tq=tq, tk=tk, lam_init=lam_init),
        grid=(B_HEADS, s // tq),
        in_specs=[
            pl.BlockSpec((tq, LANES), lambda h, i: (i, qb + h)),
            pl.BlockSpec((s, LANES), lambda h, i: (0, kb + h)),
            pl.BlockSpec((s, LANES), lambda h, i: (0, vb + h)),
            pl.BlockSpec((tq, LANES), lambda h, i: (i, gb + h)),
            pl.BlockSpec((None, 1, 1), lambda h, i: (h, 0, 0)),
            pl.BlockSpec((4, HEAD_DIM), lambda h, i: (0, 0)),
            pl.BlockSpec((1, LANES), lambda h, i: (0, 0)),
        ],
        out_specs=pl.BlockSpec((tq, LANES), lambda h, i: (i, h)),
        out_shape=jax.ShapeDtypeStruct((s, B_WIDTH), BF16),
        scratch_shapes=[pltpu.VMEM((tk, LANES), BF16)] + _flash_scratch(s, tq, tk, LANES),
        compiler_params=_cparams(("parallel", "arbitrary")),
    )(z, z, z, z, slopes.reshape(B_HEADS, 1, 1), lam_params, subln_g.reshape(1, LANES))


def _swa_kernel(slope_ref, sink_ref, q_ref, kp_ref, kc_ref, vp_ref, vc_ref, *rest):
    *g_refs, o_ref = rest
    i = pl.program_id(0)
    blk = WINDOW
    lane = lax.broadcasted_iota(jnp.int32, (1, LANES), 1)
    lo = lane < HEAD_DIM

    def swap_halves(a):
        return pltpu.roll(a.astype(F32), HEAD_DIM, 1).astype(BF16)

    k = jnp.concatenate([kp_ref[...], kc_ref[...]], axis=0)
    v = jnp.concatenate([vp_ref[...], vc_ref[...]], axis=0)
    k_sw, v_sw = swap_halves(k), swap_halves(v)

    r = lax.broadcasted_iota(jnp.int32, (blk, 2 * blk), 0)
    c = lax.broadcasted_iota(jnp.int32, (blk, 2 * blk), 1)
    dist = r - c + blk
    valid = (dist >= 0) & (dist < WINDOW) & ((c >= blk) | (i > 0))
    neg_dist = jnp.where(valid, -dist.astype(F32), NEG_INF)

    for pair in range(A_HEADS // 2):
        qp = q_ref[:, pair * LANES:(pair + 1) * LANES] * jnp.asarray(SCALE, BF16)
        zero = jnp.zeros_like(qp)
        outs = []
        for a in range(2):
            h = 2 * pair + a
            kv = h // A_GROUP
            qh = jnp.where(lo, qp, zero) if a == 0 else jnp.where(lo, zero, qp)
            kk, vv = (k, v) if a == kv else (k_sw, v_sw)
            s = lax.dot_general(qh, kk, (((1,), (1,)), ((), ())), preferred_element_type=F32)
            s = s + slope_ref[h] * neg_dist
            sink = sink_ref[h]
            m = jnp.maximum(jnp.max(s, axis=-1, keepdims=True), sink)
            e = jnp.exp(s - m)
            denom = jnp.sum(e, axis=-1, keepdims=True) + jnp.exp(sink - m)
            outs.append(jnp.dot(e.astype(BF16), vv, preferred_element_type=F32) / denom)
        y = jnp.where(lo, outs[0], outs[1])
        g_ref = g_refs[pair // 2]
        g = g_ref[:, (pair % 2) * LANES:(pair % 2 + 1) * LANES].astype(F32)
        o_ref[:, pair * LANES:(pair + 1) * LANES] = (y * _silu(g)).astype(o_ref.dtype)


def _swa_attention(z, slopes, sinks):
    s = z.shape[0]
    blk = WINDOW
    nq = A_WIDTH // LANES
    kb, vb = nq, nq + 1
    gw = 2 * LANES
    g_first = (nq + 2) * LANES // gw
    assert g_first * gw == (nq + 2) * LANES
    smem = pl.BlockSpec(memory_space=pltpu.SMEM)
    prev = lambda i: (jnp.maximum(i - 1, 0), kb)
    prev_v = lambda i: (jnp.maximum(i - 1, 0), vb)
    return pl.pallas_call(
        _swa_kernel,
        grid=(s // blk,),
        in_specs=[
            smem, smem,
            pl.BlockSpec((blk, A_WIDTH), lambda i: (i, 0)),
            pl.BlockSpec((blk, LANES), prev),
            pl.BlockSpec((blk, LANES), lambda i: (i, kb)),
            pl.BlockSpec((blk, LANES), prev_v),
            pl.BlockSpec((blk, LANES), lambda i: (i, vb)),
            *[pl.BlockSpec((blk, gw), lambda i, c=g_first + n: (i, c)) for n in range(A_WIDTH // gw)],
        ],
        out_specs=pl.BlockSpec((blk, A_WIDTH), lambda i: (i, 0)),
        out_shape=jax.ShapeDtypeStruct((s, A_WIDTH), BF16),
        compiler_params=_cparams(("parallel",)),
    )(slopes, sinks, *([z] * (5 + A_WIDTH // gw)))


def _out_kernel(*refs, n_y, final):
    x_ref = refs[0]
    y_refs = refs[1:1 + n_y]
    w_refs = refs[1 + n_y:1 + 2 * n_y]
    p_ref, wp_ref, gn_ref, wg_ref = refs[1 + 2 * n_y:5 + 2 * n_y]
    if final:
        fg_ref, o_ref = refs[5 + 2 * n_y:]
    else:
        (o_ref,) = refs[5 + 2 * n_y:]

    x1 = x_ref[...]
    for y_ref, w_ref in zip(y_refs, w_refs):
        x1 = x1 + jnp.dot(y_ref[...], w_ref[...], preferred_element_type=F32)
    hn = x1 * lax.rsqrt(jnp.mean(x1 * x1, axis=-1, keepdims=True) + RMS_EPS) * gn_ref[...]
    gate = jax.nn.sigmoid(jnp.dot(hn.astype(BF16), wg_ref[...], preferred_element_type=F32))
    pp = jnp.dot(p_ref[...].astype(BF16), wp_ref[...], preferred_element_type=F32)
    x2 = x1 + gate * pp
    if final:
        x2 = x2 * lax.rsqrt(jnp.mean(x2 * x2, axis=-1, keepdims=True) + RMS_EPS) * fg_ref[...]
    o_ref[...] = x2


def _out_block(x, ys, w, p, wp, gn, wg, final_g=None, *, tm):
    s, d = x.shape
    n_y = len(ys)
    final = final_g is not None
    const = lambda i: (0, 0)
    single = pl.Buffered(1)
    in_specs = [pl.BlockSpec((tm, d), lambda i: (i, 0))]
    in_specs += [pl.BlockSpec((tm, y.shape[1]), lambda i: (i, 0)) for y in ys]
    in_specs += [pl.BlockSpec((y.shape[1], d), lambda i, r=r: (r, 0), pipeline_mode=single)
                 for r, y in enumerate(ys)]
    assert all(y.shape[1] == ys[0].shape[1] for y in ys) and len(ys) * ys[0].shape[1] == w.shape[0]
    ws = [w] * n_y
    in_specs += [
        pl.BlockSpec((tm, p.shape[1]), lambda i: (i, 0)),
        pl.BlockSpec(wp.shape, const, pipeline_mode=single),
        pl.BlockSpec((1, d), const),
        pl.BlockSpec(wg.shape, const, pipeline_mode=single),
    ]
    args = [x, *ys, *ws, p, wp, gn.reshape(1, d), wg]
    if final:
        in_specs.append(pl.BlockSpec((1, d), const))
        args.append(final_g.reshape(1, d))
    return pl.pallas_call(
        functools.partial(_out_kernel, n_y=n_y, final=final),
        grid=(s // tm,),
        in_specs=in_specs,
        out_specs=pl.BlockSpec((tm, d), lambda i: (i, 0)),
        out_shape=jax.ShapeDtypeStruct((s, d), F32),
        compiler_params=_cparams(("parallel",)),
    )(*args)


def _alibi_slopes(n):
    return jnp.asarray([2.0 ** (-8.0 * (h + 1) / n) for h in range(n)], dtype=F32)


def kernel(x, p, norm_g, w_in_ab, w_out_ab, attn_sinks, diff_lambda, diff_subln_g,
           w_in_c, w_out_c, forget_bias, ple_proj, ple_gate, ple_norm_g, final_norm_g):
    b, s, d = x.shape
    assert (b, s, d) == (1, SEQ, D_MODEL)
    xs = x.reshape(s, d)
    for i in range(DEPTH):
        j = i // 2
        last = i == DEPTH - 1
        if i % 2 == 0:
            lam_init = 0.8 - 0.6 * math.exp(-0.3 * i)
            tn = 1280
            z = _norm_matmul(xs, norm_g[i], [(w_in_ab[j].astype(BF16), 0, AB_IN // tn)], tm=1024, tn=tn)
            ya = _swa_attention(z, _alibi_slopes(A_HEADS), attn_sinks[j].astype(F32))
            yb = _diff_attention(z, _alibi_slopes(B_HEADS), diff_lambda[j].astype(F32),
                                 diff_subln_g[j].astype(F32), lam_init, tq=1024, tk=512)
            ys, w_out = [ya, yb], w_out_ab[j].astype(BF16)
        else:
            tn = 1024
            w = w_in_c[j].astype(BF16)
            w_g = w_in_c[j][:, 3 * C_WIDTH + C_HEADS:].astype(BF16)
            z, f = _norm_matmul(xs, norm_g[i], [(w, 0, 3 * C_WIDTH // tn), (w_g, 0, C_WIDTH // tn)],
                                (w, 3 * C_WIDTH // LANES), tm=1024, tn=tn)
            f_bias = jnp.pad(forget_bias[j].astype(F32), (0, LANES - C_HEADS)).reshape(1, LANES)
            cb = _logsig_cumsum(f, f_bias, C_HEADS // 2)
            ys, w_out = [_fox_attention(z, cb, tq=1024, tk=512)], w_out_c[j].astype(BF16)
        xs = _out_block(xs, ys, w_out, p[i].reshape(s, D_PLE), ple_proj[i].astype(BF16),
                        ple_norm_g[i], ple_gate[i].astype(BF16),
                        final_norm_g if last else None, tm=256)
    return xs.reshape(b, s, d)
```

```python
import functools
import itertools
import math

import jax
import jax.numpy as jnp
from jax import lax
from jax.experimental import pallas as pl
from jax.experimental.pallas import tpu as pltpu

F32 = jnp.float32
BF16 = jnp.bfloat16

D_MODEL = 2048
SEQ = 8192
DEPTH = 2
HEAD_DIM = 64
D_PLE = 256
WINDOW = 128
RMS_EPS = 1e-6
A_HEADS = 16
A_KV_HEADS = 2
A_GROUP = A_HEADS // A_KV_HEADS
A_WIDTH = A_HEADS * HEAD_DIM
B_HEADS = 8
B_WIDTH = B_HEADS * 2 * HEAD_DIM
C_HEADS = 32
C_WIDTH = C_HEADS * HEAD_DIM
AB_IN = 6400
SCALE = HEAD_DIM ** -0.5

LANES = 128
NEG_INF = float("-inf")
VMEM_LIMIT = 48 * 1024 * 1024


def _cparams(sem):
    return pltpu.CompilerParams(dimension_semantics=sem, vmem_limit_bytes=VMEM_LIMIT)


def _silu(g):
    return g * (1.0 / (1.0 + jnp.exp(-g)))


NORM_ROWS = 256


def _norm_matmul_kernel(x_ref, g_ref, *rest, seg_ends, with_extra):
    w_refs, rest = rest[:len(seg_ends)], rest[len(seg_ends):]
    if with_extra:
        we_ref, o_ref, oe_ref, h_ref = rest
    else:
        o_ref, h_ref = rest
    j = pl.program_id(1)

    @pl.when(j == 0)
    def _():
        def norm_rows(r, carry):
            rows = pl.ds(pl.multiple_of(r * NORM_ROWS, NORM_ROWS), NORM_ROWS)
            x = x_ref[rows, :]
            ms = jnp.mean(x * x, axis=-1, keepdims=True)
            h_ref[rows, :] = (x * lax.rsqrt(ms + RMS_EPS) * g_ref[...]).astype(BF16)
            return carry
        lax.fori_loop(0, x_ref.shape[0] // NORM_ROWS, norm_rows, 0)
        if with_extra:
            oe_ref[...] = jnp.dot(h_ref[...], we_ref[...], preferred_element_type=F32)

    def project(w_ref):
        o_ref[...] = jnp.dot(h_ref[...], w_ref[...], preferred_element_type=F32).astype(o_ref.dtype)

    if len(w_refs) == 1:
        project(w_refs[0])
    else:
        for lo, hi, w_ref in zip((0,) + seg_ends[:-1], seg_ends, w_refs):
            pl.when((j >= lo) & (j < hi))(functools.partial(project, w_ref))


def _norm_matmul(x, g, segments, extra=None, *, tm, tn):
    s, d = x.shape
    seg_ends = tuple(itertools.accumulate(count for _, _, count in segments))
    with_extra = extra is not None
    in_specs = [
        pl.BlockSpec((tm, d), lambda i, j: (i, 0)),
        pl.BlockSpec((1, d), lambda i, j: (0, 0)),
    ]
    args = [x, g.reshape(1, d)]
    for (w, first, count), end in zip(segments, seg_ends):
        in_specs.append(pl.BlockSpec(
            (d, tn), lambda i, j, first=first, count=count, lo=end - count:
            (0, first + jnp.clip(j - lo, 0, count - 1))))
        args.append(w)
    n = seg_ends[-1] * tn
    out_shape = [jax.ShapeDtypeStruct((s, n), BF16)]
    out_specs = [pl.BlockSpec((tm, tn), lambda i, j: (i, j))]
    if with_extra:
        we, tile = extra
        in_specs.append(pl.BlockSpec((d, LANES), lambda i, j: (0, tile)))
        out_shape.append(jax.ShapeDtypeStruct((s, LANES), F32))
        out_specs.append(pl.BlockSpec((tm, LANES), lambda i, j: (i, 0)))
        args.append(we)
    res = pl.pallas_call(
        functools.partial(_norm_matmul_kernel, seg_ends=seg_ends, with_extra=with_extra),
        grid=(s // tm, n // tn),
        in_specs=in_specs,
        out_specs=out_specs,
        out_shape=out_shape,
        scratch_shapes=[pltpu.VMEM((tm, d), BF16)],
        compiler_params=_cparams(("parallel", "arbitrary")),
    )(*args)
    return res if with_extra else res[0]


N_SPLIT = 3


def _split_bf16(x):
    parts = []
    for _ in range(N_SPLIT):
        t = x.astype(BF16)
        parts.append(t)
        x = x - t.astype(F32)
    return parts


def _logsig_cumsum_kernel(f_ref, b_ref, o_ref, carry_ref, sel_ref, *, blk):
    n = f_ref.shape[1]

    @pl.when(pl.program_id(0) == 0)
    def _():
        carry_ref[...] = jnp.zeros_like(carry_ref)
        r = lax.broadcasted_iota(jnp.int32, sel_ref.shape, 0)
        l = lax.broadcasted_iota(jnp.int32, sel_ref.shape, 1)
        part, head = r // n, r % n
        lane = jnp.where(l // LANES == head // 2, l % LANES, -1)
        sel_ref[...] = jnp.where(lane == N_SPLIT * (head % 2) + part, 1.0, 0.0).astype(BF16)

    z = f_ref[...] + b_ref[...]
    ls = jnp.minimum(z, 0.0) - jnp.log1p(jnp.exp(-jnp.abs(z)))
    row = lax.broadcasted_iota(jnp.int32, (blk, blk), 0)
    col = lax.broadcasted_iota(jnp.int32, (blk, blk), 1)
    lower = (col <= row).astype(F32)
    c = jnp.dot(lower, ls, preferred_element_type=F32,
                precision=lax.Precision.HIGHEST) + carry_ref[...]
    carry_ref[...] = c[blk - 1:blk, :]
    terms = jnp.concatenate(_split_bf16(c), axis=1)
    o_ref[...] = jnp.dot(terms, sel_ref[...], preferred_element_type=F32).astype(BF16)


def _logsig_cumsum(f, bias_row, n_pairs, *, blk=256):
    s, n = f.shape
    return pl.pallas_call(
        functools.partial(_logsig_cumsum_kernel, blk=blk),
        grid=(s // blk,),
        in_specs=[pl.BlockSpec((blk, n), lambda i: (i, 0)),
                  pl.BlockSpec((1, n), lambda i: (0, 0))],
        out_specs=pl.BlockSpec((blk, n_pairs * LANES), lambda i: (i, 0)),
        out_shape=jax.ShapeDtypeStruct((s, n_pairs * LANES), BF16),
        scratch_shapes=[pltpu.VMEM((1, n), F32), pltpu.VMEM((N_SPLIT * n, n_pairs * LANES), BF16)],
        compiler_params=_cparams(("arbitrary",)),
    )(f, bias_row)


ONES_ROWS = 16
CHUNK = 256
MIN_DENOMINATOR = 1e-17
LOG_SKIP = 94.0


def _flash_scratch(s, tq, tk, rows):
    return [
        pltpu.VMEM((s // tk, LANES + ONES_ROWS, tk), BF16),
        pltpu.VMEM((2, 2 * LANES, tq), BF16),
        pltpu.VMEM((2, 1, tq), F32),
        pltpu.VMEM((2, rows + ONES_ROWS, tq), F32),
        pltpu.VMEM((2, s // tk, 1), F32),
        *[pltpu.VMEM((tk, tq), F32) for _ in range(4)],
        *[pltpu.VMEM((1, tq), F32) for _ in range(4)],
        *[pltpu.VMEM((tk, tq), BF16) for _ in range(4)],
        *[pltpu.VMEM((1, tq), F32) for _ in range(4)],
    ]


def _fill_vt(v_ref, vt_ref, tk):
    def body(c, carry):
        vt_ref[c, :LANES, :] = v_ref[pl.ds(pl.multiple_of(c * tk, tk), tk), :].T
        vt_ref[c, LANES:, :] = jnp.ones((ONES_ROWS, tk), BF16)
        return carry
    lax.fori_loop(0, vt_ref.shape[0], body, 0)


def _map_lanes(m):
    lane = lax.broadcasted_iota(jnp.int32, (1, LANES), 1)
    return (lane < HEAD_DIM) if m == 0 else (lane >= HEAD_DIM)


def _max_sq_norm(x, m):
    x = x.astype(F32)
    rows = jnp.sum(jnp.where(_map_lanes(m), x * x, 0.0), axis=1, keepdims=True)
    return jnp.max(rows, axis=0, keepdims=True)


def _fill_key_norms(k_ref, kn_ref, tk):
    r = lax.broadcasted_iota(jnp.int32, (LANES, LANES), 0)
    c = lax.broadcasted_iota(jnp.int32, (LANES, LANES), 1)
    sel = jnp.where(c == jnp.where(r < HEAD_DIM, 0, 1), 1.0, 0.0).astype(BF16)
    tiles = []
    for j in range(kn_ref.shape[1]):
        kt = k_ref[j * tk:(j + 1) * tk, :].astype(F32)
        sq = kt * kt
        hi = sq.astype(BF16)
        lo = (sq - hi.astype(F32)).astype(BF16)
        sums = (jnp.dot(hi, sel, preferred_element_type=F32) + jnp.dot(lo, sel, preferred_element_type=F32))
        tiles.append(jnp.max(sums, axis=0, keepdims=True) * (1.0 + 2.0 ** -14))
    tiles = jnp.concatenate(tiles, axis=0)
    for m in range(2):
        kn_ref[m] = tiles[:, m:m + 1]


def _pair_flash(q_ref, k_ref, kb_fn, off_fn, decay_fn, bias_row_fn, finish, i, scratch, *, tq, tk, v_rows,
                bias_rows, bias_sign, split_maps):
    assert tq == 2 * tk
    vt_ref, qt_ref, m_ref, acc_ref, kn_ref = scratch[:5]
    s_ref, mt_ref, p_ref, a_ref = ({(slot, h): scratch[5 + 4 * n + 2 * slot + h]
                                    for slot in range(2) for h in range(2)} for n in range(4))
    row = lax.broadcasted_iota(jnp.int32, (LANES, 1), 0)
    lo = row < HEAD_DIM
    qt = (q_ref[...] * jnp.asarray(SCALE, BF16)).T
    zero = jnp.zeros_like(qt)
    for h in range(2):
        qt_ref[h, :LANES, :] = jnp.where(lo, qt, zero) if h == 0 else jnp.where(lo, zero, qt)
        b0, b1 = bias_rows[h]
        qt_ref[h, LANES:, :] = jnp.where((row >= b0) & (row < b1), bias_sign, 0.0).astype(BF16) + zero

    lanes = [(h, slice(c, c + CHUNK)) for c in range(0, tq, CHUNK) for h in range(2)]

    def tile_of(n):
        return jnp.where(n < 2, 2 * i + n, n - 2)

    def raw_scores(j, diag, h, cs):
        j0 = pl.multiple_of(j * tk, tk)
        lhs = jnp.concatenate([k_ref[pl.ds(j0, tk), :], kb_fn(j0)], axis=1)
        t = jnp.dot(lhs, qt_ref[h, :, cs], preferred_element_type=F32)
        if diag is not None:
            kpos = lax.broadcasted_iota(jnp.int32, t.shape, 0) + diag * tk
            qpos = lax.broadcasted_iota(jnp.int32, t.shape, 1) + cs.start
            t = jnp.where(kpos <= qpos, t, NEG_INF)
        return t

    def scores(j, slot, diag, h, cs):
        t = raw_scores(j, diag, h, cs)
        s_ref[slot, h][:, cs] = t
        mt_ref[slot, h][:, cs] = jnp.max(t, axis=0, keepdims=True)

    def values(j, slot, h, cs, first=False, rescale=True):
        r0, r1 = v_rows[h]
        vt = vt_ref[j, r0:r1, :]
        if r1 != LANES + ONES_ROWS:
            vt = jnp.concatenate([vt, vt_ref[j, LANES:, :]], axis=0)
        pv = jnp.dot(vt, p_ref[slot, h][:, cs], preferred_element_type=F32)
        if first:
            acc_ref[h, :, cs] = pv
        elif rescale:
            acc_ref[h, :, cs] = acc_ref[h, :, cs] * a_ref[slot, h][:, cs] + pv
        else:
            acc_ref[h, :, cs] = acc_ref[h, :, cs] + pv

    def softmax(j, slot, h, cs):
        off = off_fn(j)
        mt = mt_ref[slot, h][:, cs]
        m_old = m_ref[h, :, cs]
        m_new = jnp.maximum(m_old, mt if off is None else mt + off)
        a_ref[1 - slot, h][:, cs] = jnp.exp(m_old - m_new)
        m_ref[h, :, cs] = m_new
        shift = m_new if off is None else m_new - off
        p_ref[1 - slot, h][:, cs] = jnp.exp(s_ref[slot, h][:, cs] - shift).astype(BF16)

    def exact_step(t, slot, diag=None, first=False, last=False):
        for h, cs in lanes:
            if not first:
                values(tile_of(t - 1), slot, h, cs)
            softmax(tile_of(t), slot, h, cs)
            if not last:
                scores(tile_of(t + 1), 1 - slot, diag, h, cs)

    def exact_sweep():
        for h in range(2):
            m_ref[h] = jnp.full((1, tq), NEG_INF, F32)
            acc_ref[h] = jnp.zeros(acc_ref.shape[1:], F32)
        for h, cs in lanes:
            scores(2 * i, 0, 0, h, cs)
        exact_step(0, 0, 1, first=True)

        def two_steps(u, carry):
            exact_step(2 * u + 1, 1)
            exact_step(2 * u + 2, 0)
            return carry

        lax.fori_loop(0, i, two_steps, 0)
        exact_step(2 * i + 1, 1, last=True)
        for h, cs in lanes:
            values(tile_of(2 * i + 1), 0, h, cs)

    def fused(j, slot, diag, h, cs):
        off = off_fn(j)
        ub = m_ref[h, :, cs]
        t = raw_scores(j, diag, h, cs)
        p_ref[slot, h][:, cs] = jnp.exp(t - (ub if off is None else ub - off)).astype(BF16)

    def first_needed_tile(maps, qn_min, kmax):
        n_tiles = kn_ref.shape[1]
        jt = lax.broadcasted_iota(jnp.int32, (n_tiles, 1), 0)
        needed = jnp.zeros((n_tiles, 1), F32)
        for m in maps:
            kn = jnp.sqrt(kn_ref[m])
            slack = qn_min[m] * (kmax[m] - kn)
            needed = jnp.maximum(needed, jnp.where(decay_fn(m) + slack <= LOG_SKIP, 1.0, 0.0))
        first = jnp.min(jnp.where((needed > 0.0) & (jt < 2 * i), jt, 2 * i).astype(F32))
        return first.astype(jnp.int32)

    def bounded_step(prev, cur, slot, maps):
        for h, cs in lanes:
            if h in maps:
                values(prev, 1 - slot, h, cs, rescale=False)
                fused(cur, slot, None, h, cs)

    def bounded_sweep():
        d0, d1 = 2 * i, 2 * i + 1
        high = lambda cs: cs.stop > tk
        qsq = jnp.square(qt.astype(F32))
        qn_min, kmax = [], []
        for m in range(2):
            qn = jnp.sqrt(jnp.sum(qsq[m * HEAD_DIM:(m + 1) * HEAD_DIM], axis=0, keepdims=True))
            kmax.append(jnp.sqrt(jnp.max(kn_ref[m], axis=0, keepdims=True)))
            qn_min.append(jnp.min(qn, axis=1, keepdims=True))
            m_ref[m] = bias_row_fn(m) + qn * kmax[m] * (1.0 + 2.0 ** -14)
        for h, cs in lanes:
            fused(d1 if high(cs) else d0, 1, 1 if high(cs) else 0, h, cs)
        for h, cs in lanes:
            values(d1 if high(cs) else d0, 1, h, cs, first=True)
            if high(cs):
                fused(d0, 0, None, h, cs)
            else:
                p_ref[0, h][:, cs] = jnp.zeros((tk, CHUNK), BF16)

        groups = ((0,), (1,)) if split_maps else ((0, 1),)
        starts = [first_needed_tile(maps, qn_min, kmax) // 2 for maps in groups]
        pending = {}
        for maps, start in zip(groups, starts):
            def two_steps(u, prev, maps=maps):
                bounded_step(prev, 2 * u, 1, maps)
                bounded_step(2 * u, 2 * u + 1, 0, maps)
                return 2 * u + 1

            prev = lax.fori_loop(start, i, two_steps, d0)
            pending.update({m: prev for m in maps})
        for h, cs in lanes:
            values(pending[h], 0, h, cs, rescale=False)
            if h == 1:
                finish(acc_ref, cs)

    bounded_sweep()
    den = acc_ref.shape[1] - ONES_ROWS
    smallest = jnp.minimum(jnp.min(acc_ref[0, den:den + 1, :]), jnp.min(acc_ref[1, den:den + 1, :]))

    @pl.when(jnp.logical_not(smallest >= MIN_DENOMINATOR))
    def _():
        exact_sweep()
        for c in range(0, tq, CHUNK):
            finish(acc_ref, slice(c, c + CHUNK))


def _fox_kernel(q_ref, k_ref, v_ref, g_ref, cb_ref, o_ref, ce_ref, *scratch, tq, tk):
    i = pl.program_id(1)
    lane = lax.broadcasted_iota(jnp.int32, (1, LANES), 1)
    pack = 16

    def c_at(rows, last, m):
        terms = cb_ref[rows, :].astype(F32)[pack - 1:pack, :] if last else cb_ref[rows, :].astype(F32)[0:1, :]
        mine = (lane >= N_SPLIT * m) & (lane < N_SPLIT * (m + 1))
        return jnp.sum(jnp.where(mine, terms, 0.0), axis=1, keepdims=True)

    @pl.when(i == 0)
    def _():
        _fill_vt(v_ref, scratch[0], tk)
        _fill_key_norms(k_ref, scratch[4], tk)
        for m in range(2):
            ce_ref[m] = jnp.concatenate([c_at(slice((j + 1) * tk - pack, (j + 1) * tk), True, m)
                                         for j in range(ce_ref.shape[1])], axis=0)

    def decay_fn(m):
        return ce_ref[m] - c_at(pl.ds(pl.multiple_of(i * tq, tq), pack), False, m)

    def bias_row_fn(m):
        r = lax.broadcasted_iota(jnp.int32, (8, LANES), 0)
        sel = jnp.where((r == m) & (lane >= N_SPLIT * m) & (lane < N_SPLIT * (m + 1)), 1.0, 0.0).astype(BF16)
        c_rows = lax.dot_general(sel, cb_ref[pl.ds(pl.multiple_of(i * tq, tq), tq), :],
                                 (((1,), (1,)), ((), ())), preferred_element_type=F32)
        return -c_rows[m:m + 1, :]

    def finish(acc_ref, cs):
        d = HEAD_DIM
        y = jnp.concatenate([acc_ref[0, :d, cs] / acc_ref[0, d:d + 1, cs],
                             acc_ref[1, :d, cs] / acc_ref[1, d:d + 1, cs]], axis=0).T
        o_ref[cs, :] = (y * _silu(g_ref[cs, :].astype(F32))).astype(o_ref.dtype)

    _pair_flash(q_ref, k_ref, lambda j0: cb_ref[pl.ds(j0, tk), :], lambda j: None, decay_fn, bias_row_fn, finish,
                i, scratch,
                tq=tq, tk=tk, v_rows=((0, HEAD_DIM), (HEAD_DIM, LANES)),
                bias_rows=((0, N_SPLIT), (N_SPLIT, 2 * N_SPLIT)), bias_sign=-1.0, split_maps=True)


def _fox_attention(z, cb, *, tq, tk):
    s = z.shape[0]
    nb = C_WIDTH // LANES
    return pl.pallas_call(
        functools.partial(_fox_kernel, tq=tq, tk=tk),
        grid=(nb, s // tq),
        in_specs=[
            pl.BlockSpec((tq, LANES), lambda p, i: (i, p)),
            pl.BlockSpec((s, LANES), lambda p, i: (0, nb + p)),
            pl.BlockSpec((s, LANES), lambda p, i: (0, 2 * nb + p)),
            pl.BlockSpec((tq, LANES), lambda p, i: (i, 3 * nb + p)),
            pl.BlockSpec((s, LANES), lambda p, i: (0, p)),
        ],
        out_specs=pl.BlockSpec((tq, LANES), lambda p, i: (i, p)),
        out_shape=jax.ShapeDtypeStruct((s, C_WIDTH), BF16),
        scratch_shapes=[pltpu.VMEM((2, s // tk, 1), F32)] + _flash_scratch(s, tq, tk, HEAD_DIM),
        compiler_params=_cparams(("parallel", "arbitrary")),
    )(z, z, z, z, cb)


def _diff_kernel(q_ref, k_ref, v_ref, g_ref, slope_ref, lam_ref, sg_ref, o_ref, kb_ref, *scratch,
                 tq, tk, lam_init):
    i = pl.program_id(1)
    slope = slope_ref[...]

    @pl.when(i == 0)
    def _():
        _fill_vt(v_ref, scratch[0], tk)
        _fill_key_norms(k_ref, scratch[4], tk)
        r = lax.broadcasted_iota(jnp.int32, (tk, LANES), 0).astype(F32)
        lane = lax.broadcasted_iota(jnp.int32, (tk, LANES), 1)
        terms = _split_bf16(slope * r)
        kb = jnp.zeros((tk, LANES), F32)
        for n, t in enumerate(terms):
            kb = jnp.where(lane == n, t.astype(F32), kb)
        kb_ref[...] = kb.astype(BF16)

    def off_fn(j):
        return slope * (j * tk - i * tq).astype(F32)

    def decay_fn(m):
        ends = lax.broadcasted_iota(jnp.int32, (scratch[4].shape[1], 1), 0) * tk + (tk - 1)
        return slope * (i * tq - ends).astype(F32)

    def bias_row_fn(m):
        return slope * lax.broadcasted_iota(jnp.int32, (1, tq), 1).astype(F32)

    lp = lam_ref[...]
    lam = (jnp.exp(jnp.sum(lp[0:1] * lp[1:2], axis=-1, keepdims=True))
           - jnp.exp(jnp.sum(lp[2:3] * lp[3:4], axis=-1, keepdims=True)) + lam_init)

    def finish(acc_ref, cs):
        d = LANES
        o = (acc_ref[0, :d, cs] / acc_ref[0, d:d + 1, cs]
             - lam * (acc_ref[1, :d, cs] / acc_ref[1, d:d + 1, cs])).T
        o = o * lax.rsqrt(jnp.mean(o * o, axis=-1, keepdims=True) + RMS_EPS) * sg_ref[...]
        o = o * (1.0 - lam_init)
        o_ref[cs, :] = (o * _silu(g_ref[cs, :].astype(F32))).astype(o_ref.dtype)

    _pair_flash(q_ref, k_ref, lambda j0: kb_ref[...], off_fn, decay_fn, bias_row_fn, finish, i, scratch,
                tq=tq, tk=tk, v_rows=((0, LANES + ONES_ROWS), (0, LANES + ONES_ROWS)),
                bias_rows=((0, N_SPLIT), (0, N_SPLIT)), bias_sign=1.0, split_maps=False)


def _diff_attention(z, slopes, lam_params, subln_g, lam_init, *, tq, tk):
    s = z.shape[0]
    qb, kb, vb, gb = 18, 26, 34, 42
    return pl.pallas_call(
        functools.partial(_diff_kernel, tq=tq, tk=tk, lam_init=lam_init),
        grid=(B_HEADS, s // tq),
        in_specs=[
            pl.BlockSpec((tq, LANES), lambda h, i: (i, qb + h)),
            pl.BlockSpec((s, LANES), lambda h, i: (0, kb + h)),
            pl.BlockSpec((s, LANES), lambda h, i: (0, vb + h)),
            pl.BlockSpec((tq, LANES), lambda h, i: (i, gb + h)),
            pl.BlockSpec((None, 1, 1), lambda h, i: (h, 0, 0)),
            pl.BlockSpec((4, HEAD_DIM), lambda h, i: (0, 0)),
            pl.BlockSpec((1, LANES), lambda h, i: (0, 0)),
        ],
        out_specs=pl.BlockSpec((tq, LANES), lambda h, i: (i, h)),
        out_shape=jax.ShapeDtypeStruct((s, B_WIDTH), BF16),
        scratch_shapes=[pltpu.VMEM((tk, LANES), BF16)] + _flash_scratch(s, tq, tk, LANES),
        compiler_params=_cparams(("parallel", "arbitrary")),
    )(z, z, z, z, slopes.reshape(B_HEADS, 1, 1), lam_params, subln_g.reshape(1, LANES))


def _swa_kernel(slope_ref, sink_ref, q_ref, kp_ref, kc_ref, vp_ref, vc_ref, *rest):
    *g_refs, o_ref = rest
    i = pl.program_id(0)
    blk = WINDOW
    lane = lax.broadcasted_iota(jnp.int32, (1, LANES), 1)
    lo = lane < HEAD_DIM

    def swap_halves(a):
        return pltpu.roll(a.astype(F32), HEAD_DIM, 1).astype(BF16)

    k = jnp.concatenate([kp_ref[...], kc_ref[...]], axis=0)
    v = jnp.concatenate([vp_ref[...], vc_ref[...]], axis=0)
    k_sw, v_sw = swap_halves(k), swap_halves(v)

    r = lax.broadcasted_iota(jnp.int32, (blk, 2 * blk), 0)
    c = lax.broadcasted_iota(jnp.int32, (blk, 2 * blk), 1)
    dist = r - c + blk
    valid = (dist >= 0) & (dist < WINDOW) & ((c >= blk) | (i > 0))
    neg_dist = jnp.where(valid, -dist.astype(F32), NEG_INF)

    for pair in range(A_HEADS // 2):
        qp = q_ref[:, pair * LANES:(pair + 1) * LANES] * jnp.asarray(SCALE, BF16)
        zero = jnp.zeros_like(qp)
        outs = []
        for a in range(2):
            h = 2 * pair + a
            kv = h // A_GROUP
            qh = jnp.where(lo, qp, zero) if a == 0 else jnp.where(lo, zero, qp)
            kk, vv = (k, v) if a == kv else (k_sw, v_sw)
            s = lax.dot_general(qh, kk, (((1,), (1,)), ((), ())), preferred_element_type=F32)
            s = s + slope_ref[h] * neg_dist
            sink = sink_ref[h]
            m = jnp.maximum(jnp.max(s, axis=-1, keepdims=True), sink)
            e = jnp.exp(s - m)
            denom = jnp.sum(e, axis=-1, keepdims=True) + jnp.exp(sink - m)
            outs.append(jnp.dot(e.astype(BF16), vv, preferred_element_type=F32) / denom)
        y = jnp.where(lo, outs[0], outs[1])
        g_ref = g_refs[pair // 2]
        g = g_ref[:, (pair % 2) * LANES:(pair % 2 + 1) * LANES].astype(F32)
        o_ref[:, pair * LANES:(pair + 1) * LANES] = (y * _silu(g)).astype(o_ref.dtype)


def _swa_attention(z, slopes, sinks):
    s = z.shape[0]
    blk = WINDOW
    nq = A_WIDTH // LANES
    kb, vb = nq, nq + 1
    gw = 2 * LANES
    g_first = (nq + 2) * LANES // gw
    assert g_first * gw == (nq + 2) * LANES
    smem = pl.BlockSpec(memory_space=pltpu.SMEM)
    prev = lambda i: (jnp.maximum(i - 1, 0), kb)
    prev_v = lambda i: (jnp.maximum(i - 1, 0), vb)
    return pl.pallas_call(
        _swa_kernel,
        grid=(s // blk,),
        in_specs=[
            smem, smem,
            pl.BlockSpec((blk, A_WIDTH), lambda i: (i, 0)),
            pl.BlockSpec((blk, LANES), prev),
            pl.BlockSpec((blk, LANES), lambda i: (i, kb)),
            pl.BlockSpec((blk, LANES), prev_v),
            pl.BlockSpec((blk, LANES), lambda i: (i, vb)),
            *[pl.BlockSpec((blk, gw), lambda i, c=g_first + n: (i, c)) for n in range(A_WIDTH // gw)],
        ],
        out_specs=pl.BlockSpec((blk, A_WIDTH), lambda i: (i, 0)),
        out_shape=jax.ShapeDtypeStruct((s, A_WIDTH), BF16),
        compiler_params=_cparams(("parallel",)),
    )(slopes, sinks, *([z] * (5 + A_WIDTH // gw)))


def _out_kernel(*refs, n_y, final):
    x_ref = refs[0]
    y_refs = refs[1:1 + n_y]
    w_refs = refs[1 + n_y:1 + 2 * n_y]
    p_ref, wp_ref, gn_ref, wg_ref = refs[1 + 2 * n_y:5 + 2 * n_y]
    if final:
        fg_ref, o_ref = refs[5 + 2 * n_y:]
    else:
        (o_ref,) = refs[5 + 2 * n_y:]

    x1 = x_ref[...]
    for y_ref, w_ref in zip(y_refs, w_refs):
        x1 = x1 + jnp.dot(y_ref[...], w_ref[...], preferred_element_type=F32)
    hn = x1 * lax.rsqrt(jnp.mean(x1 * x1, axis=-1, keepdims=True) + RMS_EPS) * gn_ref[...]
    gate = jax.nn.sigmoid(jnp.dot(hn.astype(BF16), wg_ref[...], preferred_element_type=F32))
    pp = jnp.dot(p_ref[...].astype(BF16), wp_ref[...], preferred_element_type=F32)
    x2 = x1 + gate * pp
    if final:
        x2 = x2 * lax.rsqrt(jnp.mean(x2 * x2, axis=-1, keepdims=True) + RMS_EPS) * fg_ref[...]
    o_ref[...] = x2


def _out_block(x, ys, w, p, wp, gn, wg, final_g=None, *, tm):
    s, d = x.shape
    n_y = len(ys)
    final = final_g is not None
    const = lambda i: (0, 0)
    single = pl.Buffered(1)
    in_specs = [pl.BlockSpec((tm, d), lambda i: (i, 0))]
    in_specs += [pl.BlockSpec((tm, y.shape[1]), lambda i: (i, 0)) for y in ys]
    in_specs += [pl.BlockSpec((y.shape[1], d), lambda i, r=r: (r, 0), pipeline_mode=single)
                 for r, y in enumerate(ys)]
    assert all(y.shape[1] == ys[0].shape[1] for y in ys) and len(ys) * ys[0].shape[1] == w.shape[0]
    ws = [w] * n_y
    in_specs += [
        pl.BlockSpec((tm, p.shape[1]), lambda i: (i, 0)),
        pl.BlockSpec(wp.shape, const, pipeline_mode=single),
        pl.BlockSpec((1, d), const),
        pl.BlockSpec(wg.shape, const, pipeline_mode=single),
    ]
    args = [x, *ys, *ws, p, wp, gn.reshape(1, d), wg]
    if final:
        in_specs.append(pl.BlockSpec((1, d), const))
        args.append(final_g.reshape(1, d))
    return pl.pallas_call(
        functools.partial(_out_kernel, n_y=n_y, final=final),
        grid=(s // tm,),
        in_specs=in_specs,
        out_specs=pl.BlockSpec((tm, d), lambda i: (i, 0)),
        out_shape=jax.ShapeDtypeStruct((s, d), F32),
        compiler_params=_cparams(("parallel",)),
    )(*args)


def _alibi_slopes(n):
    return jnp.asarray([2.0 ** (-8.0 * (h + 1) / n) for h in range(n)], dtype=F32)


def kernel(x, p, norm_g, w_in_ab, w_out_ab, attn_sinks, diff_lambda, diff_subln_g,
           w_in_c, w_out_c, forget_bias, ple_proj, ple_gate, ple_norm_g, final_norm_g):
    b, s, d = x.shape
    assert (b, s, d) == (1, SEQ, D_MODEL)
    xs = x.reshape(s, d)
    for i in range(DEPTH):
        j = i // 2
        last = i == DEPTH - 1
        if i % 2 == 0:
            lam_init = 0.8 - 0.6 * math.exp(-0.3 * i)
            tn = 1280
            z = _norm_matmul(xs, norm_g[i], [(w_in_ab[j].astype(BF16), 0, AB_IN // tn)], tm=1024, tn=tn)
            ya = _swa_attention(z, _alibi_slopes(A_HEADS), attn_sinks[j].astype(F32))
            yb = _diff_attention(z, _alibi_slopes(B_HEADS), diff_lambda[j].astype(F32),
                                 diff_subln_g[j].astype(F32), lam_init, tq=1024, tk=512)
            ys, w_out = [ya, yb], w_out_ab[j].astype(BF16)
        else:
            tn = 1024
            w = w_in_c[j].astype(BF16)
            w_g = w_in_c[j][:, 3 * C_WIDTH + C_HEADS:].astype(BF16)
            z, f = _norm_matmul(xs, norm_g[i], [(w, 0, 3 * C_WIDTH // tn), (w_g, 0, C_WIDTH // tn)],
                                (w, 3 * C_WIDTH // LANES), tm=1024, tn=tn)
            f_bias = jnp.pad(forget_bias[j].astype(F32), (0, LANES - C_HEADS)).reshape(1, LANES)
            cb = _logsig_cumsum(f, f_bias, C_HEADS // 2)
            ys, w_out = [_fox_attention(z, cb, tq=1024, tk=512)], w_out_c[j].astype(BF16)
        xs = _out_block(xs, ys, w_out, p[i].reshape(s, D_PLE), ple_proj[i].astype(BF16),
                        ple_norm_g[i], ple_gate[i].astype(BF16),
                        final_norm_g if last else None, tm=256)
    return xs.reshape(b, s, d)
```

```python
import functools
import itertools
import math

import jax
import jax.numpy as jnp
from jax import lax
from jax.experimental import pallas as pl
from jax.experimental.pallas import tpu as pltpu

F32 = jnp.float32
BF16 = jnp.bfloat16

D_MODEL = 2048
SEQ = 8192
DEPTH = 2
HEAD_DIM = 64
D_PLE = 256
WINDOW = 128
RMS_EPS = 1e-6
A_HEADS = 16
A_KV_HEADS = 2
A_GROUP = A_HEADS // A_KV_HEADS
A_WIDTH = A_HEADS * HEAD_DIM
B_HEADS = 8
B_WIDTH = B_HEADS * 2 * HEAD_DIM
C_HEADS = 32
C_WIDTH = C_HEADS * HEAD_DIM
AB_IN = 6400
SCALE = HEAD_DIM ** -0.5

LANES = 128
NEG_INF = float("-inf")
VMEM_LIMIT = 48 * 1024 * 1024


def _cparams(sem):
    return pltpu.CompilerParams(dimension_semantics=sem, vmem_limit_bytes=VMEM_LIMIT)


def _silu(g):
    return g * (1.0 / (1.0 + jnp.exp(-g)))


NORM_ROWS = 256


def _norm_matmul_kernel(x_ref, g_ref, *rest, seg_ends, with_extra):
    w_refs, rest = rest[:len(seg_ends)], rest[len(seg_ends):]
    if with_extra:
        we_ref, o_ref, oe_ref, h_ref = rest
    else:
        o_ref, h_ref = rest
    j = pl.program_id(1)

    @pl.when(j == 0)
    def _():
        def norm_rows(r, carry):
            rows = pl.ds(pl.multiple_of(r * NORM_ROWS, NORM_ROWS), NORM_ROWS)
            x = x_ref[rows, :]
            ms = jnp.mean(x * x, axis=-1, keepdims=True)
            h_ref[rows, :] = (x * lax.rsqrt(ms + RMS_EPS) * g_ref[...]).astype(BF16)
            return carry
        lax.fori_loop(0, x_ref.shape[0] // NORM_ROWS, norm_rows, 0)
        if with_extra:
            oe_ref[...] = jnp.dot(h_ref[...], we_ref[...], preferred_element_type=F32)

    def project(w_ref):
        o_ref[...] = jnp.dot(h_ref[...], w_ref[...], preferred_element_type=F32).astype(o_ref.dtype)

    if len(w_refs) == 1:
        project(w_refs[0])
    else:
        for lo, hi, w_ref in zip((0,) + seg_ends[:-1], seg_ends, w_refs):
            pl.when((j >= lo) & (j < hi))(functools.partial(project, w_ref))


def _norm_matmul(x, g, segments, extra=None, *, tm, tn):
    s, d = x.shape
    seg_ends = tuple(itertools.accumulate(count for _, _, count in segments))
    with_extra = extra is not None
    in_specs = [
        pl.BlockSpec((tm, d), lambda i, j: (i, 0)),
        pl.BlockSpec((1, d), lambda i, j: (0, 0)),
    ]
    args = [x, g.reshape(1, d)]
    for (w, first, count), end in zip(segments, seg_ends):
        in_specs.append(pl.BlockSpec(
            (d, tn), lambda i, j, first=first, count=count, lo=end - count:
            (0, first + jnp.clip(j - lo, 0, count - 1))))
        args.append(w)
    n = seg_ends[-1] * tn
    out_shape = [jax.ShapeDtypeStruct((s, n), BF16)]
    out_specs = [pl.BlockSpec((tm, tn), lambda i, j: (i, j))]
    if with_extra:
        we, tile = extra
        in_specs.append(pl.BlockSpec((d, LANES), lambda i, j: (0, tile)))
        out_shape.append(jax.ShapeDtypeStruct((s, LANES), F32))
        out_specs.append(pl.BlockSpec((tm, LANES), lambda i, j: (i, 0)))
        args.append(we)
    res = pl.pallas_call(
        functools.partial(_norm_matmul_kernel, seg_ends=seg_ends, with_extra=with_extra),
        grid=(s // tm, n // tn),
        in_specs=in_specs,
        out_specs=out_specs,
        out_shape=out_shape,
        scratch_shapes=[pltpu.VMEM((tm, d), BF16)],
        compiler_params=_cparams(("parallel", "arbitrary")),
    )(*args)
    return res if with_extra else res[0]


N_SPLIT = 3


def _split_bf16(x):
    parts = []
    for _ in range(N_SPLIT):
        t = x.astype(BF16)
        parts.append(t)
        x = x - t.astype(F32)
    return parts


def _logsig_cumsum_kernel(f_ref, b_ref, o_ref, carry_ref, sel_ref, *, blk):
    n = f_ref.shape[1]

    @pl.when(pl.program_id(0) == 0)
    def _():
        carry_ref[...] = jnp.zeros_like(carry_ref)
        r = lax.broadcasted_iota(jnp.int32, sel_ref.shape, 0)
        l = lax.broadcasted_iota(jnp.int32, sel_ref.shape, 1)
        part, head = r // n, r % n
        lane = jnp.where(l // LANES == head // 2, l % LANES, -1)
        sel_ref[...] = jnp.where(lane == N_SPLIT * (head % 2) + part, 1.0, 0.0).astype(BF16)

    z = f_ref[...] + b_ref[...]
    ls = jnp.minimum(z, 0.0) - jnp.log1p(jnp.exp(-jnp.abs(z)))
    row = lax.broadcasted_iota(jnp.int32, (blk, blk), 0)
    col = lax.broadcasted_iota(jnp.int32, (blk, blk), 1)
    lower = (col <= row).astype(F32)
    c = jnp.dot(lower, ls, preferred_element_type=F32,
                precision=lax.Precision.HIGHEST) + carry_ref[...]
    carry_ref[...] = c[blk - 1:blk, :]
    terms = jnp.concatenate(_split_bf16(c), axis=1)
    o_ref[...] = jnp.dot(terms, sel_ref[...], preferred_element_type=F32).astype(BF16)


def _logsig_cumsum(f, bias_row, n_pairs, *, blk=256):
    s, n = f.shape
    return pl.pallas_call(
        functools.partial(_logsig_cumsum_kernel, blk=blk),
        grid=(s // blk,),
        in_specs=[pl.BlockSpec((blk, n), lambda i: (i, 0)),
                  pl.BlockSpec((1, n), lambda i: (0, 0))],
        out_specs=pl.BlockSpec((blk, n_pairs * LANES), lambda i: (i, 0)),
        out_shape=jax.ShapeDtypeStruct((s, n_pairs * LANES), BF16),
        scratch_shapes=[pltpu.VMEM((1, n), F32), pltpu.VMEM((N_SPLIT * n, n_pairs * LANES), BF16)],
        compiler_params=_cparams(("arbitrary",)),
    )(f, bias_row)


ONES_ROWS = 16
CHUNK = 256
MIN_DENOMINATOR = 1e-17
LOG_SKIP = 94.0


def _flash_scratch(s, tq, tk, rows):
    return [
        pltpu.VMEM((s // tk, LANES + ONES_ROWS, tk), BF16),
        pltpu.VMEM((2, 2 * LANES, tq), BF16),
        pltpu.VMEM((2, 1, tq), F32),
        pltpu.VMEM((2, rows + ONES_ROWS, tq), F32),
        pltpu.VMEM((2, s // tk, 1), F32),
        *[pltpu.VMEM((tk, tq), F32) for _ in range(4)],
        *[pltpu.VMEM((1, tq), F32) for _ in range(4)],
        *[pltpu.VMEM((tk, tq), BF16) for _ in range(4)],
        *[pltpu.VMEM((1, tq), F32) for _ in range(4)],
    ]


def _fill_vt(v_ref, vt_ref, tk):
    def body(c, carry):
        vt_ref[c, :LANES, :] = v_ref[pl.ds(pl.multiple_of(c * tk, tk), tk), :].T
        vt_ref[c, LANES:, :] = jnp.ones((ONES_ROWS, tk), BF16)
        return carry
    lax.fori_loop(0, vt_ref.shape[0], body, 0)


def _fill_key_norms(k_ref, kn_ref, tk):
    r = lax.broadcasted_iota(jnp.int32, (LANES, LANES), 0)
    c = lax.broadcasted_iota(jnp.int32, (LANES, LANES), 1)
    sel = jnp.where(c == jnp.where(r < HEAD_DIM, 0, 1), 1.0, 0.0).astype(BF16)
    tiles = []
    for j in range(kn_ref.shape[1]):
        kt = k_ref[j * tk:(j + 1) * tk, :].astype(F32)
        sq = kt * kt
        hi = sq.astype(BF16)
        lo = (sq - hi.astype(F32)).astype(BF16)
        sums = (jnp.dot(hi, sel, preferred_element_type=F32) + jnp.dot(lo, sel, preferred_element_type=F32))
        tiles.append(jnp.max(sums, axis=0, keepdims=True) * (1.0 + 2.0 ** -14))
    tiles = jnp.concatenate(tiles, axis=0)
    for m in range(2):
        kn_ref[m] = tiles[:, m:m + 1]


def _pair_flash(q_ref, k_ref, kb_fn, off_fn, decay_fn, bias_row_fn, finish, i, scratch, *, tq, tk, v_rows,
                bias_rows, bias_sign, split_maps):
    assert tq == 2 * tk
    vt_ref, qt_ref, m_ref, acc_ref, kn_ref = scratch[:5]
    s_ref, mt_ref, p_ref, a_ref = ({(slot, h): scratch[5 + 4 * n + 2 * slot + h]
                                    for slot in range(2) for h in range(2)} for n in range(4))
    row = lax.broadcasted_iota(jnp.int32, (LANES, 1), 0)
    lo = row < HEAD_DIM
    qt = (q_ref[...] * jnp.asarray(SCALE, BF16)).T
    zero = jnp.zeros_like(qt)
    for h in range(2):
        qt_ref[h, :LANES, :] = jnp.where(lo, qt, zero) if h == 0 else jnp.where(lo, zero, qt)
        b0, b1 = bias_rows[h]
        qt_ref[h, LANES:, :] = jnp.where((row >= b0) & (row < b1), bias_sign, 0.0).astype(BF16) + zero

    lanes = [(h, slice(c, c + CHUNK)) for c in range(0, tq, CHUNK) for h in range(2)]

    def tile_of(n):
        return jnp.where(n < 2, 2 * i + n, n - 2)

    def raw_scores(j, diag, h, cs, rows=tk):
        j0 = pl.multiple_of(j * tk, tk)
        lhs = jnp.concatenate([k_ref[pl.ds(j0, rows), :], kb_fn(j0)[:rows]], axis=1)
        t = jnp.dot(lhs, qt_ref[h, :, cs], preferred_element_type=F32)
        if diag is not None:
            kpos = lax.broadcasted_iota(jnp.int32, t.shape, 0) + diag * tk
            qpos = lax.broadcasted_iota(jnp.int32, t.shape, 1) + cs.start
            t = jnp.where(kpos <= qpos, t, NEG_INF)
        return t

    def scores(j, slot, diag, h, cs):
        t = raw_scores(j, diag, h, cs)
        s_ref[slot, h][:, cs] = t
        mt_ref[slot, h][:, cs] = jnp.max(t, axis=0, keepdims=True)

    def values(j, slot, h, cs, first=False, rescale=True, rows=tk):
        r0, r1 = v_rows[h]
        vt = vt_ref[j, r0:r1, :rows]
        if r1 != LANES + ONES_ROWS:
            vt = jnp.concatenate([vt, vt_ref[j, LANES:, :rows]], axis=0)
        pv = jnp.dot(vt, p_ref[slot, h][:rows, cs], preferred_element_type=F32)
        if first:
            acc_ref[h, :, cs] = pv
        elif rescale:
            acc_ref[h, :, cs] = acc_ref[h, :, cs] * a_ref[slot, h][:, cs] + pv
        else:
            acc_ref[h, :, cs] = acc_ref[h, :, cs] + pv

    def softmax(j, slot, h, cs):
        off = off_fn(j)
        mt = mt_ref[slot, h][:, cs]
        m_old = m_ref[h, :, cs]
        m_new = jnp.maximum(m_old, mt if off is None else mt + off)
        a_ref[1 - slot, h][:, cs] = jnp.exp(m_old - m_new)
        m_ref[h, :, cs] = m_new
        shift = m_new if off is None else m_new - off
        p_ref[1 - slot, h][:, cs] = jnp.exp(s_ref[slot, h][:, cs] - shift).astype(BF16)

    def exact_step(t, slot, diag=None, first=False, last=False):
        for h, cs in lanes:
            if not first:
                values(tile_of(t - 1), slot, h, cs)
            softmax(tile_of(t), slot, h, cs)
            if not last:
                scores(tile_of(t + 1), 1 - slot, diag, h, cs)

    def exact_sweep():
        for h in range(2):
            m_ref[h] = jnp.full((1, tq), NEG_INF, F32)
            acc_ref[h] = jnp.zeros(acc_ref.shape[1:], F32)
        for h, cs in lanes:
            scores(2 * i, 0, 0, h, cs)
        exact_step(0, 0, 1, first=True)

        def two_steps(u, carry):
            exact_step(2 * u + 1, 1)
            exact_step(2 * u + 2, 0)
            return carry

        lax.fori_loop(0, i, two_steps, 0)
        exact_step(2 * i + 1, 1, last=True)
        for h, cs in lanes:
            values(tile_of(2 * i + 1), 0, h, cs)

    def fused(j, slot, diag, h, cs, rows=tk):
        off = off_fn(j)
        ub = m_ref[h, :, cs]
        t = raw_scores(j, diag, h, cs, rows)
        p_ref[slot, h][:rows, cs] = jnp.exp(t - (ub if off is None else ub - off)).astype(BF16)

    def first_needed_tile(maps, qn_min, kmax):
        n_tiles = kn_ref.shape[1]
        jt = lax.broadcasted_iota(jnp.int32, (n_tiles, 1), 0)
        needed = jnp.zeros((n_tiles, 1), F32)
        for m in maps:
            kn = jnp.sqrt(kn_ref[m])
            slack = qn_min[m] * (kmax[m] - kn)
            needed = jnp.maximum(needed, jnp.where(decay_fn(m) + slack <= LOG_SKIP, 1.0, 0.0))
        first = jnp.min(jnp.where((needed > 0.0) & (jt < 2 * i), jt, 2 * i).astype(F32))
        return first.astype(jnp.int32)

    def bounded_step(prev, cur, slot, maps):
        for h, cs in lanes:
            if h in maps:
                values(prev, 1 - slot, h, cs, rescale=False)
                fused(cur, slot, None, h, cs)

    def bounded_sweep():
        d0, d1 = 2 * i, 2 * i + 1
        high = lambda cs: cs.stop > tk
        qsq = jnp.square(qt.astype(F32))
        qn_min, kmax = [], []
        for m in range(2):
            qn = jnp.sqrt(jnp.sum(qsq[m * HEAD_DIM:(m + 1) * HEAD_DIM], axis=0, keepdims=True))
            kmax.append(jnp.sqrt(jnp.max(kn_ref[m], axis=0, keepdims=True)))
            qn_min.append(jnp.min(qn, axis=1, keepdims=True))
            m_ref[m] = bias_row_fn(m) + qn * kmax[m] * (1.0 + 2.0 ** -14)
        reach = lambda cs: CHUNK if cs.start % tk == 0 else tk
        for h, cs in lanes:
            fused(d1 if high(cs) else d0, 1, 1 if high(cs) else 0, h, cs, reach(cs))
        for h, cs in lanes:
            values(d1 if high(cs) else d0, 1, h, cs, first=True, rows=reach(cs))
            if high(cs):
                fused(d0, 0, None, h, cs)
            else:
                p_ref[0, h][:, cs] = jnp.zeros((tk, CHUNK), BF16)

        groups = ((0,), (1,)) if split_maps else ((0, 1),)
        starts = [first_needed_tile(maps, qn_min, kmax) // 2 for maps in groups]
        pending = {}
        for maps, start in zip(groups, starts):
            def two_steps(u, prev, maps=maps):
                bounded_step(prev, 2 * u, 1, maps)
                bounded_step(2 * u, 2 * u + 1, 0, maps)
                return 2 * u + 1

            prev = lax.fori_loop(start, i, two_steps, d0)
            pending.update({m: prev for m in maps})
        for h, cs in lanes:
            values(pending[h], 0, h, cs, rescale=False)
            if h == 1:
                finish(acc_ref, cs)

    bounded_sweep()
    den = acc_ref.shape[1] - ONES_ROWS
    smallest = jnp.minimum(jnp.min(acc_ref[0, den:den + 1, :]), jnp.min(acc_ref[1, den:den + 1, :]))

    @pl.when(jnp.logical_not(smallest >= MIN_DENOMINATOR))
    def _():
        exact_sweep()
        for c in range(0, tq, CHUNK):
            finish(acc_ref, slice(c, c + CHUNK))


def _fox_kernel(q_ref, k_ref, v_ref, g_ref, cb_ref, o_ref, ce_ref, *scratch, tq, tk):
    i = pl.program_id(1)
    lane = lax.broadcasted_iota(jnp.int32, (1, LANES), 1)
    pack = 16

    def c_at(rows, last, m):
        terms = cb_ref[rows, :].astype(F32)[pack - 1:pack, :] if last else cb_ref[rows, :].astype(F32)[0:1, :]
        mine = (lane >= N_SPLIT * m) & (lane < N_SPLIT * (m + 1))
        return jnp.sum(jnp.where(mine, terms, 0.0), axis=1, keepdims=True)

    @pl.when(i == 0)
    def _():
        _fill_vt(v_ref, scratch[0], tk)
        _fill_key_norms(k_ref, scratch[4], tk)
        for m in range(2):
            ce_ref[m] = jnp.concatenate([c_at(slice((j + 1) * tk - pack, (j + 1) * tk), True, m)
                                         for j in range(ce_ref.shape[1])], axis=0)

    def decay_fn(m):
        return ce_ref[m] - c_at(pl.ds(pl.multiple_of(i * tq, tq), pack), False, m)

    def bias_row_fn(m):
        r = lax.broadcasted_iota(jnp.int32, (8, LANES), 0)
        sel = jnp.where((r == m) & (lane >= N_SPLIT * m) & (lane < N_SPLIT * (m + 1)), 1.0, 0.0).astype(BF16)
        c_rows = lax.dot_general(sel, cb_ref[pl.ds(pl.multiple_of(i * tq, tq), tq), :],
                                 (((1,), (1,)), ((), ())), preferred_element_type=F32)
        return -c_rows[m:m + 1, :]

    def finish(acc_ref, cs):
        d = HEAD_DIM
        y = jnp.concatenate([acc_ref[0, :d, cs] / acc_ref[0, d:d + 1, cs],
                             acc_ref[1, :d, cs] / acc_ref[1, d:d + 1, cs]], axis=0).T
        o_ref[cs, :] = (y * _silu(g_ref[cs, :].astype(F32))).astype(o_ref.dtype)

    _pair_flash(q_ref, k_ref, lambda j0: cb_ref[pl.ds(j0, tk), :], lambda j: None, decay_fn, bias_row_fn, finish,
                i, scratch,
                tq=tq, tk=tk, v_rows=((0, HEAD_DIM), (HEAD_DIM, LANES)),
                bias_rows=((0, N_SPLIT), (N_SPLIT, 2 * N_SPLIT)), bias_sign=-1.0, split_maps=True)


def _fox_attention(z, cb, *, tq, tk):
    s = z.shape[0]
    nb = C_WIDTH // LANES
    return pl.pallas_call(
        functools.partial(_fox_kernel, tq=tq, tk=tk),
        grid=(nb, s // tq),
        in_specs=[
            pl.BlockSpec((tq, LANES), lambda p, i: (i, p)),
            pl.BlockSpec((s, LANES), lambda p, i: (0, nb + p)),
            pl.BlockSpec((s, LANES), lambda p, i: (0, 2 * nb + p)),
            pl.BlockSpec((tq, LANES), lambda p, i: (i, 3 * nb + p)),
            pl.BlockSpec((s, LANES), lambda p, i: (0, p)),
        ],
        out_specs=pl.BlockSpec((tq, LANES), lambda p, i: (i, p)),
        out_shape=jax.ShapeDtypeStruct((s, C_WIDTH), BF16),
        scratch_shapes=[pltpu.VMEM((2, s // tk, 1), F32)] + _flash_scratch(s, tq, tk, HEAD_DIM),
        compiler_params=_cparams(("parallel", "arbitrary")),
    )(z, z, z, z, cb)


def _diff_kernel(q_ref, k_ref, v_ref, g_ref, slope_ref, lam_ref, sg_ref, o_ref, kb_ref, *scratch,
                 tq, tk, lam_init):
    i = pl.program_id(1)
    slope = slope_ref[...]

    @pl.when(i == 0)
    def _():
        _fill_vt(v_ref, scratch[0], tk)
        _fill_key_norms(k_ref, scratch[4], tk)
        r = lax.broadcasted_iota(jnp.int32, (tk, LANES), 0).astype(F32)
        lane = lax.broadcasted_iota(jnp.int32, (tk, LANES), 1)
        terms = _split_bf16(slope * r)
        kb = jnp.zeros((tk, LANES), F32)
        for n, t in enumerate(terms):
            kb = jnp.where(lane == n, t.astype(F32), kb)
        kb_ref[...] = kb.astype(BF16)

    def off_fn(j):
        return slope * (j * tk - i * tq).astype(F32)

    def decay_fn(m):
        ends = lax.broadcasted_iota(jnp.int32, (scratch[4].shape[1], 1), 0) * tk + (tk - 1)
        return slope * (i * tq - ends).astype(F32)

    def bias_row_fn(m):
        return slope * lax.broadcasted_iota(jnp.int32, (1, tq), 1).astype(F32)

    lp = lam_ref[...]
    lam = (jnp.exp(jnp.sum(lp[0:1] * lp[1:2], axis=-1, keepdims=True))
           - jnp.exp(jnp.sum(lp[2:3] * lp[3:4], axis=-1, keepdims=True)) + lam_init)

    def finish(acc_ref, cs):
        d = LANES
        o = (acc_ref[0, :d, cs] / acc_ref[0, d:d + 1, cs]
             - lam * (acc_ref[1, :d, cs] / acc_ref[1, d:d + 1, cs])).T
        o = o * lax.rsqrt(jnp.mean(o * o, axis=-1, keepdims=True) + RMS_EPS) * sg_ref[...]
        o = o * (1.0 - lam_init)
        o_ref[cs, :] = (o * _silu(g_ref[cs, :].astype(F32))).astype(o_ref.dtype)

    _pair_flash(q_ref, k_ref, lambda j0: kb_ref[...], off_fn, decay_fn, bias_row_fn, finish, i, scratch,
                tq=tq, tk=tk, v_rows=((0, LANES + ONES_ROWS), (0, LANES + ONES_ROWS)),
                bias_rows=((0, N_SPLIT), (0, N_SPLIT)), bias_sign=1.0, split_maps=False)


def _diff_attention(z, slopes, lam_params, subln_g, lam_init, *, tq, tk):
    s = z.shape[0]
    nb = B_WIDTH // LANES
    qb = (2 * A_WIDTH + 2 * A_KV_HEADS * HEAD_DIM) // LANES
    kb, vb, gb = qb + nb, qb + 2 * nb, qb + 3 * nb
    return pl.pallas_call(
        functools.partial(_diff_kernel, tq=tq, tk=tk, lam_init=lam_init),
        grid=(B_HEADS, s // tq),
        in_specs=[
            pl.BlockSpec((tq, LANES), lambda h, i: (i, qb + h)),
            pl.BlockSpec((s, LANES), lambda h, i: (0, kb + h)),
            pl.BlockSpec((s, LANES), lambda h, i: (0, vb + h)),
            pl.BlockSpec((tq, LANES), lambda h, i: (i, gb + h)),
            pl.BlockSpec((None, 1, 1), lambda h, i: (h, 0, 0)),
            pl.BlockSpec((4, HEAD_DIM), lambda h, i: (0, 0)),
            pl.BlockSpec((1, LANES), lambda h, i: (0, 0)),
        ],
        out_specs=pl.BlockSpec((tq, LANES), lambda h, i: (i, h)),
        out_shape=jax.ShapeDtypeStruct((s, B_WIDTH), BF16),
        scratch_shapes=[pltpu.VMEM((tk, LANES), BF16)] + _flash_scratch(s, tq, tk, LANES),
        compiler_params=_cparams(("parallel", "arbitrary")),
    )(z, z, z, z, slopes.reshape(B_HEADS, 1, 1), lam_params, subln_g.reshape(1, LANES))


def _swa_kernel(slope_ref, sink_ref, q_ref, kp_ref, kc_ref, vp_ref, vc_ref, *rest):
    *g_refs, o_ref = rest
    i = pl.program_id(0)
    blk = WINDOW
    lane = lax.broadcasted_iota(jnp.int32, (1, LANES), 1)
    lo = lane < HEAD_DIM

    def swap_halves(a):
        return pltpu.roll(a.astype(F32), HEAD_DIM, 1).astype(BF16)

    k = jnp.concatenate([kp_ref[...], kc_ref[...]], axis=0)
    v = jnp.concatenate([vp_ref[...], vc_ref[...]], axis=0)
    k_sw, v_sw = swap_halves(k), swap_halves(v)

    r = lax.broadcasted_iota(jnp.int32, (blk, 2 * blk), 0)
    c = lax.broadcasted_iota(jnp.int32, (blk, 2 * blk), 1)
    dist = r - c + blk
    valid = (dist >= 0) & (dist < WINDOW) & ((c >= blk) | (i > 0))
    neg_dist = jnp.where(valid, -dist.astype(F32), NEG_INF)

    for pair in range(A_HEADS // 2):
        qp = q_ref[:, pair * LANES:(pair + 1) * LANES] * jnp.asarray(SCALE, BF16)
        zero = jnp.zeros_like(qp)
        outs = []
        for a in range(2):
            h = 2 * pair + a
            kv = h // A_GROUP
            qh = jnp.where(lo, qp, zero) if a == 0 else jnp.where(lo, zero, qp)
            kk, vv = (k, v) if a == kv else (k_sw, v_sw)
            s = lax.dot_general(qh, kk, (((1,), (1,)), ((), ())), preferred_element_type=F32)
            s = s + slope_ref[h] * neg_dist
            sink = sink_ref[h]
            m = jnp.maximum(jnp.max(s, axis=-1, keepdims=True), sink)
            e = jnp.exp(s - m)
            denom = jnp.sum(e, axis=-1, keepdims=True) + jnp.exp(sink - m)
            outs.append(jnp.dot(e.astype(BF16), vv, preferred_element_type=F32) / denom)
        y = jnp.where(lo, outs[0], outs[1])
        g_ref = g_refs[pair // 2]
        g = g_ref[:, (pair % 2) * LANES:(pair % 2 + 1) * LANES].astype(F32)
        o_ref[:, pair * LANES:(pair + 1) * LANES] = (y * _silu(g)).astype(o_ref.dtype)


def _swa_attention(z, slopes, sinks):
    s = z.shape[0]
    blk = WINDOW
    nq = A_WIDTH // LANES
    kb, vb = nq, nq + 1
    gw = 2 * LANES
    g_first = (nq + 2) * LANES // gw
    assert g_first * gw == (nq + 2) * LANES
    smem = pl.BlockSpec(memory_space=pltpu.SMEM)
    prev = lambda i: (jnp.maximum(i - 1, 0), kb)
    prev_v = lambda i: (jnp.maximum(i - 1, 0), vb)
    return pl.pallas_call(
        _swa_kernel,
        grid=(s // blk,),
        in_specs=[
            smem, smem,
            pl.BlockSpec((blk, A_WIDTH), lambda i: (i, 0)),
            pl.BlockSpec((blk, LANES), prev),
            pl.BlockSpec((blk, LANES), lambda i: (i, kb)),
            pl.BlockSpec((blk, LANES), prev_v),
            pl.BlockSpec((blk, LANES), lambda i: (i, vb)),
            *[pl.BlockSpec((blk, gw), lambda i, c=g_first + n: (i, c)) for n in range(A_WIDTH // gw)],
        ],
        out_specs=pl.BlockSpec((blk, A_WIDTH), lambda i: (i, 0)),
        out_shape=jax.ShapeDtypeStruct((s, A_WIDTH), BF16),
        compiler_params=_cparams(("parallel",)),
    )(slopes, sinks, *([z] * (5 + A_WIDTH // gw)))


def _out_kernel(*refs, n_y, final):
    x_ref = refs[0]
    y_refs = refs[1:1 + n_y]
    w_refs = refs[1 + n_y:1 + 2 * n_y]
    p_ref, wp_ref, gn_ref, wg_ref = refs[1 + 2 * n_y:5 + 2 * n_y]
    if final:
        fg_ref, o_ref = refs[5 + 2 * n_y:]
    else:
        (o_ref,) = refs[5 + 2 * n_y:]

    x1 = x_ref[...]
    for y_ref, w_ref in zip(y_refs, w_refs):
        x1 = x1 + jnp.dot(y_ref[...], w_ref[...], preferred_element_type=F32)
    hn = x1 * lax.rsqrt(jnp.mean(x1 * x1, axis=-1, keepdims=True) + RMS_EPS) * gn_ref[...]
    gate = jax.nn.sigmoid(jnp.dot(hn.astype(BF16), wg_ref[...], preferred_element_type=F32))
    pp = jnp.dot(p_ref[...].astype(BF16), wp_ref[...], preferred_element_type=F32)
    x2 = x1 + gate * pp
    if final:
        x2 = x2 * lax.rsqrt(jnp.mean(x2 * x2, axis=-1, keepdims=True) + RMS_EPS) * fg_ref[...]
    o_ref[...] = x2


def _out_block(x, ys, w, p, wp, gn, wg, final_g=None, *, tm):
    s, d = x.shape
    n_y = len(ys)
    final = final_g is not None
    const = lambda i: (0, 0)
    single = pl.Buffered(1)
    in_specs = [pl.BlockSpec((tm, d), lambda i: (i, 0))]
    in_specs += [pl.BlockSpec((tm, y.shape[1]), lambda i: (i, 0)) for y in ys]
    in_specs += [pl.BlockSpec((y.shape[1], d), lambda i, r=r: (r, 0), pipeline_mode=single)
                 for r, y in enumerate(ys)]
    assert all(y.shape[1] == ys[0].shape[1] for y in ys) and len(ys) * ys[0].shape[1] == w.shape[0]
    ws = [w] * n_y
    in_specs += [
        pl.BlockSpec((tm, p.shape[1]), lambda i: (i, 0)),
        pl.BlockSpec(wp.shape, const, pipeline_mode=single),
        pl.BlockSpec((1, d), const),
        pl.BlockSpec(wg.shape, const, pipeline_mode=single),
    ]
    args = [x, *ys, *ws, p, wp, gn.reshape(1, d), wg]
    if final:
        in_specs.append(pl.BlockSpec((1, d), const))
        args.append(final_g.reshape(1, d))
    return pl.pallas_call(
        functools.partial(_out_kernel, n_y=n_y, final=final),
        grid=(s // tm,),
        in_specs=in_specs,
        out_specs=pl.BlockSpec((tm, d), lambda i: (i, 0)),
        out_shape=jax.ShapeDtypeStruct((s, d), F32),
        compiler_params=_cparams(("parallel",)),
    )(*args)


def _alibi_slopes(n):
    return jnp.asarray([2.0 ** (-8.0 * (h + 1) / n) for h in range(n)], dtype=F32)


def kernel(x, p, norm_g, w_in_ab, w_out_ab, attn_sinks, diff_lambda, diff_subln_g,
           w_in_c, w_out_c, forget_bias, ple_proj, ple_gate, ple_norm_g, final_norm_g):
    b, s, d = x.shape
    assert (b, s, d) == (1, SEQ, D_MODEL)
    xs = x.reshape(s, d)
    for i in range(DEPTH):
        j = i // 2
        last = i == DEPTH - 1
        if i % 2 == 0:
            lam_init = 0.8 - 0.6 * math.exp(-0.3 * i)
            tn = 1280
            z = _norm_matmul(xs, norm_g[i], [(w_in_ab[j].astype(BF16), 0, AB_IN // tn)], tm=1024, tn=tn)
            ya = _swa_attention(z, _alibi_slopes(A_HEADS), attn_sinks[j].astype(F32))
            yb = _diff_attention(z, _alibi_slopes(B_HEADS), diff_lambda[j].astype(F32),
                                 diff_subln_g[j].astype(F32), lam_init, tq=1024, tk=512)
            ys, w_out = [ya, yb], w_out_ab[j].astype(BF16)
        else:
            tn = 1024
            w = w_in_c[j].astype(BF16)
            w_g = w_in_c[j][:, 3 * C_WIDTH + C_HEADS:].astype(BF16)
            z, f = _norm_matmul(xs, norm_g[i], [(w, 0, 3 * C_WIDTH // tn), (w_g, 0, C_WIDTH // tn)],
                                (w, 3 * C_WIDTH // LANES), tm=1024, tn=tn)
            f_bias = jnp.pad(forget_bias[j].astype(F32), (0, LANES - C_HEADS)).reshape(1, LANES)
            cb = _logsig_cumsum(f, f_bias, C_HEADS // 2)
            ys, w_out = [_fox_attention(z, cb, tq=1024, tk=512)], w_out_c[j].astype(BF16)
        xs = _out_block(xs, ys, w_out, p[i].reshape(s, D_PLE), ple_proj[i].astype(BF16),
                        ple_norm_g[i], ple_gate[i].astype(BF16),
                        final_norm_g if last else None, tm=256)
    return xs.reshape(b, s, d)
```

```python
import functools
import itertools
import math

import jax
import jax.numpy as jnp
from jax import lax
from jax.experimental import pallas as pl
from jax.experimental.pallas import tpu as pltpu

F32 = jnp.float32
BF16 = jnp.bfloat16

D_MODEL = 2048
SEQ = 8192
DEPTH = 2
HEAD_DIM = 64
D_PLE = 256
WINDOW = 128
RMS_EPS = 1e-6
A_HEADS = 16
A_KV_HEADS = 2
A_GROUP = A_HEADS // A_KV_HEADS
A_WIDTH = A_HEADS * HEAD_DIM
B_HEADS = 8
B_WIDTH = B_HEADS * 2 * HEAD_DIM
C_HEADS = 32
C_WIDTH = C_HEADS * HEAD_DIM
AB_IN = 6400
SCALE = HEAD_DIM ** -0.5

LANES = 128
NEG_INF = float("-inf")
VMEM_LIMIT = 48 * 1024 * 1024


def _cparams(sem):
    return pltpu.CompilerParams(dimension_semantics=sem, vmem_limit_bytes=VMEM_LIMIT)


def _silu(g):
    return g * (1.0 / (1.0 + jnp.exp(-g)))


NORM_ROWS = 256


def _norm_matmul_kernel(x_ref, g_ref, *rest, seg_ends, with_extra):
    w_refs, rest = rest[:len(seg_ends)], rest[len(seg_ends):]
    if with_extra:
        we_ref, o_ref, oe_ref, h_ref = rest
    else:
        o_ref, h_ref = rest
    j = pl.program_id(1)

    @pl.when(j == 0)
    def _():
        def norm_rows(r, carry):
            rows = pl.ds(pl.multiple_of(r * NORM_ROWS, NORM_ROWS), NORM_ROWS)
            x = x_ref[rows, :]
            ms = jnp.mean(x * x, axis=-1, keepdims=True)
            h_ref[rows, :] = (x * lax.rsqrt(ms + RMS_EPS) * g_ref[...]).astype(BF16)
            return carry
        lax.fori_loop(0, x_ref.shape[0] // NORM_ROWS, norm_rows, 0)
        if with_extra:
            oe_ref[...] = jnp.dot(h_ref[...], we_ref[...], preferred_element_type=F32)

    def project(w_ref):
        o_ref[...] = jnp.dot(h_ref[...], w_ref[...], preferred_element_type=F32).astype(o_ref.dtype)

    if len(w_refs) == 1:
        project(w_refs[0])
    else:
        for lo, hi, w_ref in zip((0,) + seg_ends[:-1], seg_ends, w_refs):
            pl.when((j >= lo) & (j < hi))(functools.partial(project, w_ref))


def _norm_matmul(x, g, segments, extra=None, *, tm, tn):
    s, d = x.shape
    seg_ends = tuple(itertools.accumulate(count for _, _, count in segments))
    with_extra = extra is not None
    in_specs = [
        pl.BlockSpec((tm, d), lambda i, j: (i, 0)),
        pl.BlockSpec((1, d), lambda i, j: (0, 0)),
    ]
    args = [x, g.reshape(1, d)]
    for (w, first, count), end in zip(segments, seg_ends):
        in_specs.append(pl.BlockSpec(
            (d, tn), lambda i, j, first=first, count=count, lo=end - count:
            (0, first + jnp.clip(j - lo, 0, count - 1))))
        args.append(w)
    n = seg_ends[-1] * tn
    out_shape = [jax.ShapeDtypeStruct((s, n), BF16)]
    out_specs = [pl.BlockSpec((tm, tn), lambda i, j: (i, j))]
    if with_extra:
        we, tile = extra
        in_specs.append(pl.BlockSpec((d, LANES), lambda i, j: (0, tile)))
        out_shape.append(jax.ShapeDtypeStruct((s, LANES), F32))
        out_specs.append(pl.BlockSpec((tm, LANES), lambda i, j: (i, 0)))
        args.append(we)
    res = pl.pallas_call(
        functools.partial(_norm_matmul_kernel, seg_ends=seg_ends, with_extra=with_extra),
        grid=(s // tm, n // tn),
        in_specs=in_specs,
        out_specs=out_specs,
        out_shape=out_shape,
        scratch_shapes=[pltpu.VMEM((tm, d), BF16)],
        compiler_params=_cparams(("parallel", "arbitrary")),
    )(*args)
    return res if with_extra else res[0]


N_SPLIT = 3


def _split_bf16(x):
    parts = []
    for _ in range(N_SPLIT):
        t = x.astype(BF16)
        parts.append(t)
        x = x - t.astype(F32)
    return parts


def _logsig_cumsum_kernel(f_ref, b_ref, o_ref, carry_ref, sel_ref, *, blk):
    n = f_ref.shape[1]

    @pl.when(pl.program_id(0) == 0)
    def _():
        carry_ref[...] = jnp.zeros_like(carry_ref)
        r = lax.broadcasted_iota(jnp.int32, sel_ref.shape, 0)
        l = lax.broadcasted_iota(jnp.int32, sel_ref.shape, 1)
        part, head = r // n, r % n
        lane = jnp.where(l // LANES == head // 2, l % LANES, -1)
        sel_ref[...] = jnp.where(lane == N_SPLIT * (head % 2) + part, 1.0, 0.0).astype(BF16)

    z = f_ref[...] + b_ref[...]
    ls = jnp.minimum(z, 0.0) - jnp.log1p(jnp.exp(-jnp.abs(z)))
    row = lax.broadcasted_iota(jnp.int32, (blk, blk), 0)
    col = lax.broadcasted_iota(jnp.int32, (blk, blk), 1)
    lower = (col <= row).astype(F32)
    c = jnp.dot(lower, ls, preferred_element_type=F32,
                precision=lax.Precision.HIGHEST) + carry_ref[...]
    carry_ref[...] = c[blk - 1:blk, :]
    terms = jnp.concatenate(_split_bf16(c), axis=1)
    o_ref[...] = jnp.dot(terms, sel_ref[...], preferred_element_type=F32).astype(BF16)


def _logsig_cumsum(f, bias_row, n_pairs, *, blk=256):
    s, n = f.shape
    return pl.pallas_call(
        functools.partial(_logsig_cumsum_kernel, blk=blk),
        grid=(s // blk,),
        in_specs=[pl.BlockSpec((blk, n), lambda i: (i, 0)),
                  pl.BlockSpec((1, n), lambda i: (0, 0))],
        out_specs=pl.BlockSpec((blk, n_pairs * LANES), lambda i: (i, 0)),
        out_shape=jax.ShapeDtypeStruct((s, n_pairs * LANES), BF16),
        scratch_shapes=[pltpu.VMEM((1, n), F32), pltpu.VMEM((N_SPLIT * n, n_pairs * LANES), BF16)],
        compiler_params=_cparams(("arbitrary",)),
    )(f, bias_row)


ONES_ROWS = 16
CHUNK = 256
MIN_DENOMINATOR = 1e-17
LOG_SKIP = 94.0


def _flash_scratch(s, tq, tk, rows):
    return [
        pltpu.VMEM((s // tk, LANES + ONES_ROWS, tk), BF16),
        pltpu.VMEM((2, 2 * LANES, tq), BF16),
        pltpu.VMEM((2, 1, tq), F32),
        pltpu.VMEM((2, rows + ONES_ROWS, tq), F32),
        pltpu.VMEM((2, s // tk, 1), F32),
        *[pltpu.VMEM((tk, tq), F32) for _ in range(4)],
        *[pltpu.VMEM((1, tq), F32) for _ in range(4)],
        *[pltpu.VMEM((tk, tq), BF16) for _ in range(4)],
        *[pltpu.VMEM((1, tq), F32) for _ in range(4)],
    ]


def _fill_vt(v_ref, vt_ref, tk):
    def body(c, carry):
        vt_ref[c, :LANES, :] = v_ref[pl.ds(pl.multiple_of(c * tk, tk), tk), :].T
        vt_ref[c, LANES:, :] = jnp.ones((ONES_ROWS, tk), BF16)
        return carry
    lax.fori_loop(0, vt_ref.shape[0], body, 0)


def _fill_key_norms(k_ref, kn_ref, tk):
    r = lax.broadcasted_iota(jnp.int32, (LANES, LANES), 0)
    c = lax.broadcasted_iota(jnp.int32, (LANES, LANES), 1)
    sel = jnp.where(c == jnp.where(r < HEAD_DIM, 0, 1), 1.0, 0.0).astype(BF16)
    tiles = []
    for j in range(kn_ref.shape[1]):
        kt = k_ref[j * tk:(j + 1) * tk, :].astype(F32)
        sq = kt * kt
        hi = sq.astype(BF16)
        lo = (sq - hi.astype(F32)).astype(BF16)
        sums = (jnp.dot(hi, sel, preferred_element_type=F32) + jnp.dot(lo, sel, preferred_element_type=F32))
        tiles.append(jnp.max(sums, axis=0, keepdims=True) * (1.0 + 2.0 ** -14))
    tiles = jnp.concatenate(tiles, axis=0)
    for m in range(2):
        kn_ref[m] = tiles[:, m:m + 1]


def _pair_flash(q_ref, k_ref, kb_fn, off_fn, decay_fn, bias_row_fn, finish, i, scratch, *, tq, tk, v_rows,
                bias_rows, bias_sign, split_maps):
    assert tq == 2 * tk
    vt_ref, qt_ref, m_ref, acc_ref, kn_ref = scratch[:5]
    s_ref, mt_ref, p_ref, a_ref = ({(slot, h): scratch[5 + 4 * n + 2 * slot + h]
                                    for slot in range(2) for h in range(2)} for n in range(4))
    row = lax.broadcasted_iota(jnp.int32, (LANES, 1), 0)
    lo = row < HEAD_DIM
    qt = (q_ref[...] * jnp.asarray(SCALE, BF16)).T
    zero = jnp.zeros_like(qt)
    for h in range(2):
        qt_ref[h, :LANES, :] = jnp.where(lo, qt, zero) if h == 0 else jnp.where(lo, zero, qt)
        b0, b1 = bias_rows[h]
        qt_ref[h, LANES:, :] = jnp.where((row >= b0) & (row < b1), bias_sign, 0.0).astype(BF16) + zero

    lanes = [(h, slice(c, c + CHUNK)) for c in range(0, tq, CHUNK) for h in range(2)]

    def tile_of(n):
        return jnp.where(n < 2, 2 * i + n, n - 2)

    def raw_scores(j, diag, h, cs, rows=tk):
        j0 = pl.multiple_of(j * tk, tk)
        lhs = jnp.concatenate([k_ref[pl.ds(j0, rows), :], kb_fn(j0)[:rows]], axis=1)
        t = jnp.dot(lhs, qt_ref[h, :, cs], preferred_element_type=F32)
        if diag is not None:
            kpos = lax.broadcasted_iota(jnp.int32, t.shape, 0) + diag * tk
            qpos = lax.broadcasted_iota(jnp.int32, t.shape, 1) + cs.start
            t = jnp.where(kpos <= qpos, t, NEG_INF)
        return t

    def scores(j, slot, diag, h, cs):
        t = raw_scores(j, diag, h, cs)
        s_ref[slot, h][:, cs] = t
        mt_ref[slot, h][:, cs] = jnp.max(t, axis=0, keepdims=True)

    def values(j, slot, h, cs, first=False, rescale=True, rows=tk):
        r0, r1 = v_rows[h]
        vt = vt_ref[j, r0:r1, :rows]
        if r1 != LANES + ONES_ROWS:
            vt = jnp.concatenate([vt, vt_ref[j, LANES:, :rows]], axis=0)
        pv = jnp.dot(vt, p_ref[slot, h][:rows, cs], preferred_element_type=F32)
        if first:
            acc_ref[h, :, cs] = pv
        elif rescale:
            acc_ref[h, :, cs] = acc_ref[h, :, cs] * a_ref[slot, h][:, cs] + pv
        else:
            acc_ref[h, :, cs] = acc_ref[h, :, cs] + pv

    def softmax(j, slot, h, cs):
        off = off_fn(j)
        mt = mt_ref[slot, h][:, cs]
        m_old = m_ref[h, :, cs]
        m_new = jnp.maximum(m_old, mt if off is None else mt + off)
        a_ref[1 - slot, h][:, cs] = jnp.exp(m_old - m_new)
        m_ref[h, :, cs] = m_new
        shift = m_new if off is None else m_new - off
        p_ref[1 - slot, h][:, cs] = jnp.exp(s_ref[slot, h][:, cs] - shift).astype(BF16)

    def exact_step(t, slot, diag=None, first=False, last=False):
        for h, cs in lanes:
            if not first:
                values(tile_of(t - 1), slot, h, cs)
            softmax(tile_of(t), slot, h, cs)
            if not last:
                scores(tile_of(t + 1), 1 - slot, diag, h, cs)

    def exact_sweep():
        for h in range(2):
            m_ref[h] = jnp.full((1, tq), NEG_INF, F32)
            acc_ref[h] = jnp.zeros(acc_ref.shape[1:], F32)
        for h, cs in lanes:
            scores(2 * i, 0, 0, h, cs)
        exact_step(0, 0, 1, first=True)

        def two_steps(u, carry):
            exact_step(2 * u + 1, 1)
            exact_step(2 * u + 2, 0)
            return carry

        lax.fori_loop(0, i, two_steps, 0)
        exact_step(2 * i + 1, 1, last=True)
        for h, cs in lanes:
            values(tile_of(2 * i + 1), 0, h, cs)

    def fused(j, slot, diag, h, cs, rows=tk):
        off = off_fn(j)
        ub = m_ref[h, :, cs]
        t = raw_scores(j, diag, h, cs, rows)
        p_ref[slot, h][:rows, cs] = jnp.exp(t - (ub if off is None else ub - off)).astype(BF16)

    def first_needed_tile(maps, qn_min, kmax):
        n_tiles = kn_ref.shape[1]
        jt = lax.broadcasted_iota(jnp.int32, (n_tiles, 1), 0)
        needed = jnp.zeros((n_tiles, 1), F32)
        for m in maps:
            kn = jnp.sqrt(kn_ref[m])
            slack = qn_min[m] * (kmax[m] - kn)
            needed = jnp.maximum(needed, jnp.where(decay_fn(m) + slack <= LOG_SKIP, 1.0, 0.0))
        first = jnp.min(jnp.where((needed > 0.0) & (jt < 2 * i), jt, 2 * i).astype(F32))
        return first.astype(jnp.int32)

    def bounded_step(prev, cur, slot, maps):
        for h, cs in lanes:
            if h in maps:
                values(prev, 1 - slot, h, cs, rescale=False)
                fused(cur, slot, None, h, cs)

    def bounded_sweep():
        d0, d1 = 2 * i, 2 * i + 1
        high = lambda cs: cs.stop > tk
        qsq = jnp.square(qt.astype(F32))
        qn_min, kmax = [], []
        for m in range(2):
            qn = jnp.sqrt(jnp.sum(qsq[m * HEAD_DIM:(m + 1) * HEAD_DIM], axis=0, keepdims=True))
            kmax.append(jnp.sqrt(jnp.max(kn_ref[m], axis=0, keepdims=True)))
            qn_min.append(jnp.min(qn, axis=1, keepdims=True))
            m_ref[m] = bias_row_fn(m) + qn * kmax[m] * (1.0 + 2.0 ** -14)
        reach = lambda cs: CHUNK if cs.start % tk == 0 else tk
        for h, cs in lanes:
            fused(d1 if high(cs) else d0, 1, 1 if high(cs) else 0, h, cs, reach(cs))
        for h, cs in lanes:
            values(d1 if high(cs) else d0, 1, h, cs, first=True, rows=reach(cs))
            if high(cs):
                fused(d0, 0, None, h, cs)
            else:
                p_ref[0, h][:, cs] = jnp.zeros((tk, CHUNK), BF16)

        groups = ((0,), (1,)) if split_maps else ((0, 1),)
        starts = [first_needed_tile(maps, qn_min, kmax) // 2 for maps in groups]
        pending = {}
        for maps, start in zip(groups, starts):
            def two_steps(u, prev, maps=maps):
                bounded_step(prev, 2 * u, 1, maps)
                bounded_step(2 * u, 2 * u + 1, 0, maps)
                return 2 * u + 1

            prev = lax.fori_loop(start, i, two_steps, d0)
            pending.update({m: prev for m in maps})
        for h, cs in lanes:
            values(pending[h], 0, h, cs, rescale=False)
            if h == 1:
                finish(acc_ref, cs)

    bounded_sweep()
    den = acc_ref.shape[1] - ONES_ROWS
    smallest = jnp.minimum(jnp.min(acc_ref[0, den:den + 1, :]), jnp.min(acc_ref[1, den:den + 1, :]))

    @pl.when(jnp.logical_not(smallest >= MIN_DENOMINATOR))
    def _():
        exact_sweep()
        for c in range(0, tq, CHUNK):
            finish(acc_ref, slice(c, c + CHUNK))


def _fox_kernel(q_ref, k_ref, v_ref, g_ref, cb_ref, o_ref, ce_ref, *scratch, tq, tk):
    i = pl.program_id(1)
    lane = lax.broadcasted_iota(jnp.int32, (1, LANES), 1)
    pack = 16

    def c_at(rows, last, m):
        terms = cb_ref[rows, :].astype(F32)[pack - 1:pack, :] if last else cb_ref[rows, :].astype(F32)[0:1, :]
        mine = (lane >= N_SPLIT * m) & (lane < N_SPLIT * (m + 1))
        return jnp.sum(jnp.where(mine, terms, 0.0), axis=1, keepdims=True)

    @pl.when(i == 0)
    def _():
        _fill_vt(v_ref, scratch[0], tk)
        _fill_key_norms(k_ref, scratch[4], tk)
        for m in range(2):
            ce_ref[m] = jnp.concatenate([c_at(slice((j + 1) * tk - pack, (j + 1) * tk), True, m)
                                         for j in range(ce_ref.shape[1])], axis=0)

    def decay_fn(m):
        return ce_ref[m] - c_at(pl.ds(pl.multiple_of(i * tq, tq), pack), False, m)

    def bias_row_fn(m):
        r = lax.broadcasted_iota(jnp.int32, (8, LANES), 0)
        sel = jnp.where((r == m) & (lane >= N_SPLIT * m) & (lane < N_SPLIT * (m + 1)), 1.0, 0.0).astype(BF16)
        c_rows = lax.dot_general(sel, cb_ref[pl.ds(pl.multiple_of(i * tq, tq), tq), :],
                                 (((1,), (1,)), ((), ())), preferred_element_type=F32)
        return -c_rows[m:m + 1, :]

    def finish(acc_ref, cs):
        d = HEAD_DIM
        y = jnp.concatenate([acc_ref[0, :d, cs] / acc_ref[0, d:d + 1, cs],
                             acc_ref[1, :d, cs] / acc_ref[1, d:d + 1, cs]], axis=0).T
        o_ref[cs, :] = (y * _silu(g_ref[cs, :].astype(F32))).astype(o_ref.dtype)

    _pair_flash(q_ref, k_ref, lambda j0: cb_ref[pl.ds(j0, tk), :], lambda j: None, decay_fn, bias_row_fn, finish,
                i, scratch,
                tq=tq, tk=tk, v_rows=((0, HEAD_DIM), (HEAD_DIM, LANES)),
                bias_rows=((0, N_SPLIT), (N_SPLIT, 2 * N_SPLIT)), bias_sign=-1.0, split_maps=True)


def _fox_attention(z, cb, *, tq, tk):
    s = z.shape[0]
    nb = C_WIDTH // LANES
    return pl.pallas_call(
        functools.partial(_fox_kernel, tq=tq, tk=tk),
        grid=(nb, s // tq),
        in_specs=[
            pl.BlockSpec((tq, LANES), lambda p, i: (i, p)),
            pl.BlockSpec((s, LANES), lambda p, i: (0, nb + p)),
            pl.BlockSpec((s, LANES), lambda p, i: (0, 2 * nb + p)),
            pl.BlockSpec((tq, LANES), lambda p, i: (i, 3 * nb + p)),
            pl.BlockSpec((s, LANES), lambda p, i: (0, p)),
        ],
        out_specs=pl.BlockSpec((tq, LANES), lambda p, i: (i, p)),
        out_shape=jax.ShapeDtypeStruct((s, C_WIDTH), BF16),
        scratch_shapes=[pltpu.VMEM((2, s // tk, 1), F32)] + _flash_scratch(s, tq, tk, HEAD_DIM),
        compiler_params=_cparams(("parallel", "arbitrary")),
    )(z, z, z, z, cb)


def _diff_kernel(q_ref, k_ref, v_ref, g_ref, slope_ref, lam_ref, sg_ref, o_ref, kb_ref, *scratch,
                 tq, tk, lam_init):
    i = pl.program_id(1)
    slope = slope_ref[...]

    @pl.when(i == 0)
    def _():
        _fill_vt(v_ref, scratch[0], tk)
        _fill_key_norms(k_ref, scratch[4], tk)
        r = lax.broadcasted_iota(jnp.int32, (tk, LANES), 0).astype(F32)
        lane = lax.broadcasted_iota(jnp.int32, (tk, LANES), 1)
        terms = _split_bf16(slope * r)
        kb = jnp.zeros((tk, LANES), F32)
        for n, t in enumerate(terms):
            kb = jnp.where(lane == n, t.astype(F32), kb)
        kb_ref[...] = kb.astype(BF16)

    def off_fn(j):
        return slope * (j * tk - i * tq).astype(F32)

    def decay_fn(m):
        ends = lax.broadcasted_iota(jnp.int32, (scratch[4].shape[1], 1), 0) * tk + (tk - 1)
        return slope * (i * tq - ends).astype(F32)

    def bias_row_fn(m):
        return slope * lax.broadcasted_iota(jnp.int32, (1, tq), 1).astype(F32)

    lp = lam_ref[...]
    lam = (jnp.exp(jnp.sum(lp[0:1] * lp[1:2], axis=-1, keepdims=True))
           - jnp.exp(jnp.sum(lp[2:3] * lp[3:4], axis=-1, keepdims=True)) + lam_init)

    def finish(acc_ref, cs):
        d = LANES
        o = (acc_ref[0, :d, cs] / acc_ref[0, d:d + 1, cs]
             - lam * (acc_ref[1, :d, cs] / acc_ref[1, d:d + 1, cs])).T
        o = o * lax.rsqrt(jnp.mean(o * o, axis=-1, keepdims=True) + RMS_EPS) * sg_ref[...]
        o = o * (1.0 - lam_init)
        o_ref[cs, :] = (o * _silu(g_ref[cs, :].astype(F32))).astype(o_ref.dtype)

    _pair_flash(q_ref, k_ref, lambda j0: kb_ref[...], off_fn, decay_fn, bias_row_fn, finish, i, scratch,
                tq=tq, tk=tk, v_rows=((0, LANES + ONES_ROWS), (0, LANES + ONES_ROWS)),
                bias_rows=((0, N_SPLIT), (0, N_SPLIT)), bias_sign=1.0, split_maps=False)


def _diff_attention(z, slopes, lam_params, subln_g, lam_init, *, tq, tk):
    s = z.shape[0]
    nb = B_WIDTH // LANES
    qb = (2 * A_WIDTH + 2 * A_KV_HEADS * HEAD_DIM) // LANES
    kb, vb, gb = qb + nb, qb + 2 * nb, qb + 3 * nb
    return pl.pallas_call(
        functools.partial(_diff_kernel, tq=tq, tk=tk, lam_init=lam_init),
        grid=(B_HEADS, s // tq),
        in_specs=[
            pl.BlockSpec((tq, LANES), lambda h, i: (i, qb + h)),
            pl.BlockSpec((s, LANES), lambda h, i: (0, kb + h)),
            pl.BlockSpec((s, LANES), lambda h, i: (0, vb + h)),
            pl.BlockSpec((tq, LANES), lambda h, i: (i, gb + h)),
            pl.BlockSpec((None, 1, 1), lambda h, i: (h, 0, 0)),
            pl.BlockSpec((4, HEAD_DIM), lambda h, i: (0, 0)),
            pl.BlockSpec((1, LANES), lambda h, i: (0, 0)),
        ],
        out_specs=pl.BlockSpec((tq, LANES), lambda h, i: (i, h)),
        out_shape=jax.ShapeDtypeStruct((s, B_WIDTH), BF16),
        scratch_shapes=[pltpu.VMEM((tk, LANES), BF16)] + _flash_scratch(s, tq, tk, LANES),
        compiler_params=_cparams(("parallel", "arbitrary")),
    )(z, z, z, z, slopes.reshape(B_HEADS, 1, 1), lam_params, subln_g.reshape(1, LANES))


def _swa_kernel(slope_ref, sink_ref, q_ref, kp_ref, kc_ref, vp_ref, vc_ref, *rest):
    *g_refs, o_ref = rest
    i = pl.program_id(0)
    blk = WINDOW
    lane = lax.broadcasted_iota(jnp.int32, (1, LANES), 1)
    lo = lane < HEAD_DIM

    def swap_halves(a):
        return pltpu.roll(a.astype(F32), HEAD_DIM, 1).astype(BF16)

    k = jnp.concatenate([kp_ref[...], kc_ref[...]], axis=0)
    v = jnp.concatenate([vp_ref[...], vc_ref[...]], axis=0)
    k_sw, v_sw = swap_halves(k), swap_halves(v)

    r = lax.broadcasted_iota(jnp.int32, (blk, 2 * blk), 0)
    c = lax.broadcasted_iota(jnp.int32, (blk, 2 * blk), 1)
    dist = r - c + blk
    valid = (dist >= 0) & (dist < WINDOW) & ((c >= blk) | (i > 0))
    neg_dist = jnp.where(valid, -dist.astype(F32), NEG_INF)

    for pair in range(A_HEADS // 2):
        qp = q_ref[:, pair * LANES:(pair + 1) * LANES] * jnp.asarray(SCALE, BF16)
        zero = jnp.zeros_like(qp)
        outs = []
        for a in range(2):
            h = 2 * pair + a
            kv = h // A_GROUP
            qh = jnp.where(lo, qp, zero) if a == 0 else jnp.where(lo, zero, qp)
            kk, vv = (k, v) if a == kv else (k_sw, v_sw)
            s = lax.dot_general(qh, kk, (((1,), (1,)), ((), ())), preferred_element_type=F32)
            s = s + slope_ref[h] * neg_dist
            sink = sink_ref[h]
            m = jnp.maximum(jnp.max(s, axis=-1, keepdims=True), sink)
            e = jnp.exp(s - m)
            denom = jnp.sum(e, axis=-1, keepdims=True) + jnp.exp(sink - m)
            outs.append(jnp.dot(e.astype(BF16), vv, preferred_element_type=F32) / denom)
        y = jnp.where(lo, outs[0], outs[1])
        g_ref = g_refs[pair // 2]
        g = g_ref[:, (pair % 2) * LANES:(pair % 2 + 1) * LANES].astype(F32)
        o_ref[:, pair * LANES:(pair + 1) * LANES] = (y * _silu(g)).astype(o_ref.dtype)


def _swa_attention(z, slopes, sinks):
    s = z.shape[0]
    blk = WINDOW
    nq = A_WIDTH // LANES
    kb, vb = nq, nq + 1
    gw = 2 * LANES
    g_first = (nq + 2) * LANES // gw
    assert g_first * gw == (nq + 2) * LANES
    smem = pl.BlockSpec(memory_space=pltpu.SMEM)
    prev = lambda i: (jnp.maximum(i - 1, 0), kb)
    prev_v = lambda i: (jnp.maximum(i - 1, 0), vb)
    return pl.pallas_call(
        _swa_kernel,
        grid=(s // blk,),
        in_specs=[
            smem, smem,
            pl.BlockSpec((blk, A_WIDTH), lambda i: (i, 0)),
            pl.BlockSpec((blk, LANES), prev),
            pl.BlockSpec((blk, LANES), lambda i: (i, kb)),
            pl.BlockSpec((blk, LANES), prev_v),
            pl.BlockSpec((blk, LANES), lambda i: (i, vb)),
            *[pl.BlockSpec((blk, gw), lambda i, c=g_first + n: (i, c)) for n in range(A_WIDTH // gw)],
        ],
        out_specs=pl.BlockSpec((blk, A_WIDTH), lambda i: (i, 0)),
        out_shape=jax.ShapeDtypeStruct((s, A_WIDTH), BF16),
        compiler_params=_cparams(("parallel",)),
    )(slopes, sinks, *([z] * (5 + A_WIDTH // gw)))


OUT_CHUNK = 512


def _out_kernel(*refs, n_y, final):
    x_ref = refs[0]
    y_refs = refs[1:1 + n_y]
    w_refs = refs[1 + n_y:1 + 2 * n_y]
    p_ref, wp_ref, gn_ref, wg_ref = refs[1 + 2 * n_y:5 + 2 * n_y]
    if final:
        fg_ref, o_ref = refs[5 + 2 * n_y:]
    else:
        (o_ref,) = refs[5 + 2 * n_y:]

    d = o_ref.shape[1]
    chunks = [slice(c, c + OUT_CHUNK) for c in range(0, d, OUT_CHUNK)]
    ssq = jnp.zeros((o_ref.shape[0], 1), F32)
    for cs in chunks:
        x1 = x_ref[:, cs]
        for y_ref, w_ref in zip(y_refs, w_refs):
            x1 = x1 + jnp.dot(y_ref[...], w_ref[:, cs], preferred_element_type=F32)
        o_ref[:, cs] = x1
        ssq = ssq + jnp.sum(x1 * x1, axis=-1, keepdims=True)
    hn = (o_ref[...] * lax.rsqrt(ssq * (1.0 / d) + RMS_EPS) * gn_ref[...]).astype(BF16)
    pb = p_ref[...].astype(BF16)
    ssq = jnp.zeros_like(ssq)
    for cs in chunks:
        gate = jax.nn.sigmoid(jnp.dot(hn, wg_ref[:, cs], preferred_element_type=F32))
        x2 = o_ref[:, cs] + gate * jnp.dot(pb, wp_ref[:, cs], preferred_element_type=F32)
        o_ref[:, cs] = x2
        ssq = ssq + jnp.sum(x2 * x2, axis=-1, keepdims=True)
    if final:
        o_ref[...] = o_ref[...] * lax.rsqrt(ssq * (1.0 / d) + RMS_EPS) * fg_ref[...]


def _out_block(x, ys, w, p, wp, gn, wg, final_g=None, *, tm):
    s, d = x.shape
    n_y = len(ys)
    final = final_g is not None
    const = lambda i: (0, 0)
    single = pl.Buffered(1)
    in_specs = [pl.BlockSpec((tm, d), lambda i: (i, 0))]
    in_specs += [pl.BlockSpec((tm, y.shape[1]), lambda i: (i, 0)) for y in ys]
    in_specs += [pl.BlockSpec((y.shape[1], d), lambda i, r=r: (r, 0), pipeline_mode=single)
                 for r, y in enumerate(ys)]
    assert all(y.shape[1] == ys[0].shape[1] for y in ys) and len(ys) * ys[0].shape[1] == w.shape[0]
    ws = [w] * n_y
    in_specs += [
        pl.BlockSpec((tm, p.shape[1]), lambda i: (i, 0)),
        pl.BlockSpec(wp.shape, const, pipeline_mode=single),
        pl.BlockSpec((1, d), const),
        pl.BlockSpec(wg.shape, const, pipeline_mode=single),
    ]
    args = [x, *ys, *ws, p, wp, gn.reshape(1, d), wg]
    if final:
        in_specs.append(pl.BlockSpec((1, d), const))
        args.append(final_g.reshape(1, d))
    return pl.pallas_call(
        functools.partial(_out_kernel, n_y=n_y, final=final),
        grid=(s // tm,),
        in_specs=in_specs,
        out_specs=pl.BlockSpec((tm, d), lambda i: (i, 0)),
        out_shape=jax.ShapeDtypeStruct((s, d), F32),
        compiler_params=_cparams(("parallel",)),
    )(*args)


def _alibi_slopes(n):
    return jnp.asarray([2.0 ** (-8.0 * (h + 1) / n) for h in range(n)], dtype=F32)


def kernel(x, p, norm_g, w_in_ab, w_out_ab, attn_sinks, diff_lambda, diff_subln_g,
           w_in_c, w_out_c, forget_bias, ple_proj, ple_gate, ple_norm_g, final_norm_g):
    b, s, d = x.shape
    assert (b, s, d) == (1, SEQ, D_MODEL)
    xs = x.reshape(s, d)
    for i in range(DEPTH):
        j = i // 2
        last = i == DEPTH - 1
        if i % 2 == 0:
            lam_init = 0.8 - 0.6 * math.exp(-0.3 * i)
            tn = 1280
            z = _norm_matmul(xs, norm_g[i], [(w_in_ab[j].astype(BF16), 0, AB_IN // tn)], tm=1024, tn=tn)
            ya = _swa_attention(z, _alibi_slopes(A_HEADS), attn_sinks[j].astype(F32))
            yb = _diff_attention(z, _alibi_slopes(B_HEADS), diff_lambda[j].astype(F32),
                                 diff_subln_g[j].astype(F32), lam_init, tq=1024, tk=512)
            ys, w_out = [ya, yb], w_out_ab[j].astype(BF16)
        else:
            tn = 1024
            w = w_in_c[j][:, :3 * C_WIDTH + LANES].astype(BF16)
            w_g = w_in_c[j][:, 3 * C_WIDTH + C_HEADS:].astype(BF16)
            z, f = _norm_matmul(xs, norm_g[i], [(w, 0, 3 * C_WIDTH // tn), (w_g, 0, C_WIDTH // tn)],
                                (w, 3 * C_WIDTH // LANES), tm=1024, tn=tn)
            f_bias = jnp.pad(forget_bias[j].astype(F32), (0, LANES - C_HEADS)).reshape(1, LANES)
            cb = _logsig_cumsum(f, f_bias, C_HEADS // 2)
            ys, w_out = [_fox_attention(z, cb, tq=1024, tk=512)], w_out_c[j].astype(BF16)
        xs = _out_block(xs, ys, w_out, p[i].reshape(s, D_PLE), ple_proj[i].astype(BF16),
                        ple_norm_g[i], ple_gate[i].astype(BF16),
                        final_norm_g if last else None, tm=512)
    return xs.reshape(b, s, d)
```
